```python
import jax, jax.numpy as jnp
from jax import lax
import numpy as np

D_MODEL = 1024
BATCH = 8
SEQ = 8192
DEPTH = 1
DEC_BATCH = 16
DEC_SEQ = 32
PAST_LEN = 1024

CHUNK = 64
CONV_WIDTH = 31
CONV_HIST = CONV_WIDTH - 1
POOL_WINDOWS = (2, 4, 8, 16)
N_POOL_GROUPS = 4
POOL_GROUP_W = D_MODEL // N_POOL_GROUPS
POOL_HIST = max(POOL_WINDOWS) - 1
N_EXPERT_GROUPS = 4
EXPERTS_PER_GROUP = 8
N_EXPERTS = N_EXPERT_GROUPS * EXPERTS_PER_GROUP
TOP_K = 2
D_EXPERT = D_MODEL // 2
MOE_BLOCK = 256
N_IN = 5
N_ADA = 6
EPS = 1e-6

kernel_name = 'hybrid_conv_pool_hmoe_stream_step'


def rmsnorm(x, g):
    x32 = x.astype(jnp.float32)
    y = x32 * lax.rsqrt(jnp.mean(x32 * x32, axis=-1, keepdims=True) + EPS)
    return (y * g.astype(jnp.float32)).astype(x.dtype)


def layernorm(x, g, b):
    x32 = x.astype(jnp.float32)
    mu = jnp.mean(x32, axis=-1, keepdims=True)
    xc = x32 - mu
    y = xc * lax.rsqrt(jnp.mean(xc * xc, axis=-1, keepdims=True) + EPS)
    return (y * g.astype(jnp.float32) + b.astype(jnp.float32)).astype(x.dtype)


def modulate(n, shift, scale):
    return n * (1 + scale[:, None, :]) + shift[:, None, :]


def conformer_conv(v_full, w_dw, b_dw, g_ln, b_ln, w_pw):
    conv = lax.conv_general_dilated(
        v_full, w_dw[:, None, :], window_strides=(1,), padding='VALID',
        dimension_numbers=('NWC', 'WIO', 'NWC'), feature_group_count=D_MODEL) + b_dw
    a = layernorm(conv, g_ln, b_ln)
    return jax.nn.silu(a) @ w_pw


def pool_mixer(u, hist, start_pos, w_pool, s_pool):
    bsz, L, _ = u.shape
    full = jnp.concatenate([hist, u], axis=1)
    cs = jnp.cumsum(full.astype(jnp.float32), axis=1)
    cs = jnp.concatenate([jnp.zeros((bsz, 1, D_MODEL), jnp.float32), cs], axis=1)
    pos = start_pos + jnp.arange(L, dtype=jnp.int32)
    means = []
    for gi, w in enumerate(POOL_WINDOWS):
        sl = slice(gi * POOL_GROUP_W, (gi + 1) * POOL_GROUP_W)
        s = cs[:, POOL_HIST + 1:POOL_HIST + 1 + L, sl] - cs[:, POOL_HIST + 1 - w:POOL_HIST + 1 - w + L, sl]
        cnt = jnp.minimum(pos + 1, w).astype(jnp.float32)
        means.append(s / cnt[None, :, None])
    mean = jnp.concatenate(means, axis=-1)
    p = (mean - u.astype(jnp.float32)).astype(u.dtype).reshape(bsz, L, N_POOL_GROUPS, POOL_GROUP_W)
    out = jnp.einsum('blgc,gcd->blgd', p, w_pool).reshape(bsz, L, D_MODEL) * s_pool
    return out, full[:, -POOL_HIST:]


def hier_moe(h, w_rg, b_rg, w_re, b_re, w_gate, w_up, w_down):
    T = h.shape[0]
    pg = jax.nn.softmax((h @ w_rg + b_rg).astype(jnp.float32), axis=-1)
    p_g, g_idx = lax.top_k(pg, 1)
    le = (h @ w_re + b_re).astype(jnp.float32).reshape(T, N_EXPERT_GROUPS, EXPERTS_PER_GROUP)
    sel = jnp.take_along_axis(le, g_idx[:, :, None], axis=1)[:, 0]
    pe = jax.nn.softmax(sel, axis=-1)
    vals, e_idx = lax.top_k(pe, TOP_K)
    wts = p_g * vals / jnp.sum(vals, axis=-1, keepdims=True)
    eid = (g_idx * EXPERTS_PER_GROUP + e_idx).astype(jnp.int32)

    TK = T * TOP_K
    flat_e = eid.reshape(-1)
    flat_w = wts.reshape(-1)
    flat_t = jnp.repeat(jnp.arange(T, dtype=jnp.int32), TOP_K)
    order = jnp.argsort(flat_e)
    se, st, sw = flat_e[order], flat_t[order], flat_w[order]
    counts = jnp.bincount(flat_e, length=N_EXPERTS).astype(jnp.int32)
    starts = jnp.cumsum(counts) - counts
    pcounts = (counts + MOE_BLOCK - 1) // MOE_BLOCK * MOE_BLOCK
    pends = jnp.cumsum(pcounts)
    pstarts = pends - pcounts
    dest = pstarts[se] + jnp.arange(TK, dtype=jnp.int32) - starts[se]
    n_blocks = (TK + N_EXPERTS * (MOE_BLOCK - 1) + MOE_BLOCK - 1) // MOE_BLOCK
    P = n_blocks * MOE_BLOCK
    xs = jnp.zeros((P, D_MODEL), h.dtype).at[dest].set(h[st])
    block_e = jnp.minimum(
        jnp.searchsorted(pends, jnp.arange(n_blocks, dtype=jnp.int32) * MOE_BLOCK, side='right'),
        N_EXPERTS - 1).astype(jnp.int32)

    def expert_block(args):
        xb, e = args
        return (jax.nn.silu(xb @ w_gate[e]) * (xb @ w_up[e])) @ w_down[e]

    ys = lax.map(expert_block, (xs.reshape(n_blocks, MOE_BLOCK, D_MODEL), block_e)).reshape(P, D_MODEL)
    return jnp.zeros((T, D_MODEL), h.dtype).at[st].add(ys[dest] * sw[:, None].astype(h.dtype))


def trunk_layer(x, c, hist_conv, hist_pool, start_pos, p):
    (w_ada, b_ada, g_mix, w_in, w_dw, b_dw, g_ln, b_ln, w_pw, w_pool, s_pool, w_out,
     g_ffn, w_rg, b_rg, w_re, b_re, w_gate, w_up, w_down) = p
    bsz, L, _ = x.shape
    mod = jax.nn.silu(c) @ w_ada + b_ada
    sh1, sc1, gt1, sh2, sc2, gt2 = jnp.split(mod, N_ADA, axis=-1)
    n = modulate(rmsnorm(x, g_mix), sh1, sc1)
    z = n @ w_in
    glu_a, glu_b, u, g_a, g_b = jnp.split(z, N_IN, axis=-1)
    v = glu_a * jax.nn.sigmoid(glu_b)
    v_full = jnp.concatenate([hist_conv, v], axis=1)
    a = conformer_conv(v_full, w_dw, b_dw, g_ln, b_ln, w_pw)
    b, new_pool = pool_mixer(u, hist_pool, start_pos, w_pool, s_pool)
    m = jax.nn.sigmoid(g_a) * a + jax.nn.sigmoid(g_b) * b
    x = x + gt1[:, None, :] * (m @ w_out)
    n2 = modulate(rmsnorm(x, g_ffn), sh2, sc2)
    f = hier_moe(n2.reshape(bsz * L, D_MODEL), w_rg, b_rg, w_re, b_re,
                 w_gate, w_up, w_down).reshape(bsz, L, D_MODEL)
    x = x + gt2[:, None, :] * f
    return x, v_full[:, -CONV_HIST:], new_pool


def setup_inputs(seed: int = 0) -> dict:
    key = jax.random.key(seed)
    ks = jax.random.split(key, 32)
    D, L, F = D_MODEL, DEPTH, D_EXPERT

    def nrm(k, shape, s):
        return jax.random.normal(k, shape, jnp.float32) * s

    return {
        'x_prompt': nrm(ks[0], (BATCH, SEQ, D), 1.0),
        'x_sample': nrm(ks[1], (DEC_BATCH, DEC_SEQ, D), 1.0),
        'state_conv': nrm(ks[2], (L, DEC_BATCH, CONV_HIST, D), 0.5),
        'state_pool': nrm(ks[3], (L, DEC_BATCH, POOL_HIST, D), 1.0),
        'c_prompt': nrm(ks[4], (BATCH, D), 1.0),
        'c_sample': nrm(ks[5], (DEC_BATCH, D), 1.0),
        'w_ada': nrm(ks[6], (L, D, N_ADA * D), 0.5 * D ** -0.5),
        'b_ada': nrm(ks[7], (L, N_ADA * D), 0.02),
        'g_mix': 1.0 + nrm(ks[8], (L, D), 0.02),
        'w_in': nrm(ks[9], (L, D, N_IN * D), D ** -0.5),
        'w_dw': nrm(ks[10], (L, CONV_WIDTH, D), CONV_WIDTH ** -0.5),
        'b_dw': nrm(ks[11], (L, D), 0.02),
        'g_ln': 1.0 + nrm(ks[12], (L, D), 0.02),
        'b_ln': nrm(ks[13], (L, D), 0.02),
        'w_pw': nrm(ks[14], (L, D, D), D ** -0.5),
        'w_pool': nrm(ks[15], (L, N_POOL_GROUPS, POOL_GROUP_W, POOL_GROUP_W), POOL_GROUP_W ** -0.5),
        's_pool': 1.0 + nrm(ks[16], (L, D), 0.1),
        'w_out': nrm(ks[17], (L, D, D), D ** -0.5),
        'g_ffn': 1.0 + nrm(ks[18], (L, D), 0.02),
        'w_rg': nrm(ks[19], (L, D, N_EXPERT_GROUPS), D ** -0.5),
        'b_rg': nrm(ks[20], (L, N_EXPERT_GROUPS), 0.01),
        'w_re': nrm(ks[21], (L, D, N_EXPERTS), D ** -0.5),
        'b_re': nrm(ks[22], (L, N_EXPERTS), 0.01),
        'w_gate': nrm(ks[23], (L, N_EXPERTS, D, F), D ** -0.5),
        'w_up': nrm(ks[24], (L, N_EXPERTS, D, F), D ** -0.5),
        'w_down': nrm(ks[25], (L, N_EXPERTS, F, D), F ** -0.5),
        'g_final': 1.0 + nrm(ks[26], (D,), 0.02),
    }


def reference(x_prompt, x_sample, state_conv, state_pool, c_prompt, c_sample,
              w_ada, b_ada, g_mix, w_in, w_dw, b_dw, g_ln, b_ln, w_pw, w_pool, s_pool,
              w_out, g_ffn, w_rg, b_rg, w_re, b_re, w_gate, w_up, w_down, g_final):
    xp, xs = x_prompt, x_sample
    conv_p, pool_p, conv_s, pool_s = [], [], [], []
    for l in range(DEPTH):
        p = (w_ada[l], b_ada[l], g_mix[l], w_in[l], w_dw[l], b_dw[l], g_ln[l], b_ln[l],
             w_pw[l], w_pool[l], s_pool[l], w_out[l], g_ffn[l], w_rg[l], b_rg[l],
             w_re[l], b_re[l], w_gate[l], w_up[l], w_down[l])
        zc = jnp.zeros((xp.shape[0], CONV_HIST, D_MODEL), xp.dtype)
        zpool = jnp.zeros((xp.shape[0], POOL_HIST, D_MODEL), xp.dtype)
        xp, ncp, npp = trunk_layer(xp, c_prompt, zc, zpool, 0, p)
        xs, ncs, nps = trunk_layer(xs, c_sample, state_conv[l], state_pool[l], PAST_LEN, p)
        conv_p.append(ncp)
        pool_p.append(npp)
        conv_s.append(ncs)
        pool_s.append(nps)
    y_prompt = rmsnorm(xp, g_final)
    y_sample = rmsnorm(xs, g_final)
    return (y_prompt, y_sample, jnp.stack(conv_p), jnp.stack(pool_p), jnp.stack(conv_s), jnp.stack(pool_s))
```

```python
import functools

import jax
import jax.numpy as jnp
from jax import lax
from jax.experimental import pallas as pl
from jax.experimental.pallas import tpu as pltpu

D = 1024
N_IN = 5
N_ADA = 6
CONV_WIDTH = 31
CONV_PAD = 32
POOL_PAD = 16
POOL_WINDOWS = (2, 4, 8, 16)
POOL_GROUP_W = D // 4
N_GROUPS = 4
EPG = 8
N_EXPERTS = N_GROUPS * EPG
N_ROUTE = 40
D_EXPERT = D // 2
EPS = 1e-6
LANES = 128
SUBLANES = 8
ROW_VREGS = D // LANES
MOE_ROWS = 128
MOE_STRIDE = MOE_ROWS + 8
VMEM_LIMIT = 56 * 1024 * 1024

_F32 = jnp.float32
_BF16 = jnp.bfloat16


def _dot(a, b):
    return jnp.dot(a, b, preferred_element_type=_F32)


def _sigmoid(x):
    return jax.nn.sigmoid(x)


def _round_bf16(x):
    return x.astype(_BF16).astype(_F32)


def _ada_kernel(c_ref, w_ref, b_ref, o_ref):
    c = c_ref[...]
    s = (c * _sigmoid(c)).astype(_BF16)
    o_ref[...] = _dot(s, w_ref[...].astype(_BF16)) + b_ref[...]


def _ada(c_all, w_ada, b_ada):
    nb = c_all.shape[0]
    return pl.pallas_call(
        _ada_kernel,
        grid=(N_ADA,),
        in_specs=[
            pl.BlockSpec((nb, D), lambda j: (0, 0)),
            pl.BlockSpec((D, D), lambda j: (0, j)),
            pl.BlockSpec((1, D), lambda j: (0, j)),
        ],
        out_specs=pl.BlockSpec((nb, D), lambda j: (0, j)),
        out_shape=jax.ShapeDtypeStruct((nb, N_ADA * D), _F32),
        name="ada",
    )(c_all, w_ada, b_ada)


def _mix_kernel(x_ref, mod_ref, hc_ref, hp_ref, gmix_ref, win_ref, wdw_ref, bdw_ref, gln_ref,
                bln_ref, wpw_ref, wpool_ref, spool_ref, wout_ref, gffn_ref, wr_ref, br_ref,
                x1_ref, n2t_ref, route_ref, ncs_ref, nps_ref,
                nb_scr, vbuf, ubuf, t1, t2, t3, t4, ab_scr, *, tl, rc, start_pos):
    l = pl.program_id(1)
    nchunk = tl // rc

    def chunks(body):
        def step(i, carry):
            body(pl.multiple_of(i * rc, rc))
            return carry
        lax.fori_loop(0, nchunk, step, 0)

    @pl.when(l == 0)
    def _():
        vbuf[0:CONV_PAD, :] = _round_bf16(hc_ref[...])
        ubuf[0:POOL_PAD, :] = hp_ref[...]

    sh1 = mod_ref[:, 0 * D:1 * D]
    sc1 = mod_ref[:, 1 * D:2 * D]
    gt1 = mod_ref[:, 2 * D:3 * D]
    sh2 = mod_ref[:, 3 * D:4 * D]
    sc2 = mod_ref[:, 4 * D:5 * D]

    def norm1(r):
        xc = x_ref[pl.ds(r, rc), :]
        ms = jnp.mean(xc * xc, axis=-1, keepdims=True)
        n = xc * lax.rsqrt(ms + EPS) * gmix_ref[...]
        nb_scr[pl.ds(r, rc), :] = (n * (1.0 + sc1) + sh1).astype(_BF16)
    chunks(norm1)

    t1[...] = _dot(nb_scr[...], win_ref[:, 0 * D:1 * D])
    t2[...] = _dot(nb_scr[...], win_ref[:, 1 * D:2 * D])

    def glu(r):
        v = t1[pl.ds(r, rc), :] * _sigmoid(t2[pl.ds(r, rc), :])
        vbuf[pl.ds(CONV_PAD + r, rc), :] = _round_bf16(v)
    chunks(glu)
    ncs_ref[...] = t1[tl - CONV_PAD:tl, :] * _sigmoid(t2[tl - CONV_PAD:tl, :])

    def conv(r):
        for c in range(ROW_VREGS):
            cs = slice(c * LANES, (c + 1) * LANES)
            win = vbuf[pl.ds(r, rc + CONV_PAD), cs]
            acc = bdw_ref[:, cs] + wdw_ref[CONV_WIDTH - 1:CONV_WIDTH, cs] * win[CONV_PAD:CONV_PAD + rc, :]
            for rr in range(SUBLANES):
                shifted = win[rr:rr + rc + CONV_PAD - SUBLANES, :]
                for q in range(CONV_PAD // SUBLANES):
                    k = q * SUBLANES + rr - (CONV_PAD - CONV_WIDTH + 1)
                    if 0 <= k < CONV_WIDTH:
                        acc = acc + wdw_ref[k:k + 1, cs] * shifted[q * SUBLANES:q * SUBLANES + rc, :]
            t1[pl.ds(r, rc), cs] = acc
        cv = t1[pl.ds(r, rc), :]
        mu = jnp.mean(cv, axis=-1, keepdims=True)
        xc = cv - mu
        a = xc * lax.rsqrt(jnp.mean(xc * xc, axis=-1, keepdims=True) + EPS) * gln_ref[...] + bln_ref[...]
        ab_scr[pl.ds(r, rc), :] = (a * _sigmoid(a)).astype(_BF16)
    chunks(conv)

    t1[...] = _dot(ab_scr[...], wpw_ref[...])

    ubuf[POOL_PAD:POOL_PAD + tl, :] = _dot(nb_scr[...], win_ref[:, 2 * D:3 * D])

    def pool(r):
        pos = start_pos + l * tl + r + lax.broadcasted_iota(jnp.int32, (rc, 1), 0)
        for g, w in enumerate(POOL_WINDOWS):
            gs = slice(g * POOL_GROUP_W, (g + 1) * POOL_GROUP_W)
            win = ubuf[pl.ds(r, rc + POOL_PAD), gs]
            u = win[POOL_PAD:POOL_PAD + rc, :]
            s = win
            shift = 1
            while shift < w:
                s = s[shift:, :] + s[:s.shape[0] - shift, :]
                shift *= 2
            off = POOL_PAD - (w - 1)
            sw = s[off:off + rc, :]
            cnt = jnp.minimum(pos + 1, w).astype(_F32)
            ab_scr[pl.ds(r, rc), gs] = (sw * (1.0 / cnt) - u).astype(_BF16)
    chunks(pool)

    for g in range(N_GROUPS):
        gs = slice(g * POOL_GROUP_W, (g + 1) * POOL_GROUP_W)
        t2[:, gs] = _dot(ab_scr[:, gs], wpool_ref[g]) * spool_ref[:, gs]

    t3[...] = _dot(nb_scr[...], win_ref[:, 3 * D:4 * D])
    t4[...] = _dot(nb_scr[...], win_ref[:, 4 * D:5 * D])

    def mix(r):
        rs = pl.ds(r, rc)
        m = _sigmoid(t3[rs, :]) * t1[rs, :] + _sigmoid(t4[rs, :]) * t2[rs, :]
        ab_scr[rs, :] = m.astype(_BF16)
    chunks(mix)

    t1[...] = _dot(ab_scr[...], wout_ref[...])

    def resid(r):
        rs = pl.ds(r, rc)
        x1 = x_ref[rs, :] + gt1 * t1[rs, :]
        x1_ref[rs, :] = x1
        ms = jnp.mean(x1 * x1, axis=-1, keepdims=True)
        n2 = x1 * lax.rsqrt(ms + EPS) * gffn_ref[...] * (1.0 + sc2) + sh2
        t2[rs, :] = n2
        nb_scr[rs, :] = n2.astype(_BF16)
    chunks(resid)

    for j in range(ROW_VREGS):
        n2t_ref[pl.ds(j, tl, stride=ROW_VREGS), :] = t2[:, j * LANES:(j + 1) * LANES]

    lg = lax.dot_general(wr_ref[...], nb_scr[...], (((1,), (1,)), ((), ())),
                         preferred_element_type=_F32) + br_ref[...]
    lgg = lg[0:N_GROUPS, :]
    mg = jnp.max(lgg, axis=0, keepdims=True)
    p_g = 1.0 / jnp.sum(jnp.exp(lgg - mg), axis=0, keepdims=True)
    gi = lax.broadcasted_iota(jnp.int32, lgg.shape, 0)
    g_idx = jnp.min(jnp.where(lgg == mg, gi, N_GROUPS), axis=0, keepdims=True)
    sel = jnp.zeros((EPG, tl), _F32)
    for g in range(N_GROUPS):
        sel = jnp.where(g_idx == g, lg[N_GROUPS + g * EPG:N_GROUPS + (g + 1) * EPG, :], sel)
    ms = jnp.max(sel, axis=0, keepdims=True)
    es = jnp.exp(sel - ms)
    pe = es / jnp.sum(es, axis=0, keepdims=True)
    ei = lax.broadcasted_iota(jnp.int32, pe.shape, 0)
    v1 = jnp.max(pe, axis=0, keepdims=True)
    i1 = jnp.min(jnp.where(pe == v1, ei, EPG), axis=0, keepdims=True)
    pe2 = jnp.where(ei == i1, -1.0, pe)
    v2 = jnp.max(pe2, axis=0, keepdims=True)
    i2 = jnp.min(jnp.where(pe2 == v2, ei, EPG), axis=0, keepdims=True)
    scale = p_g / (v1 + v2)
    route_ref[0:1, :] = (g_idx * EPG + i1).astype(_F32)
    route_ref[1:2, :] = (g_idx * EPG + i2).astype(_F32)
    route_ref[2:3, :] = v1 * scale
    route_ref[3:4, :] = v2 * scale
    route_ref[4:8, :] = jnp.zeros((4, tl), _F32)

    vtail = vbuf[tl:tl + CONV_PAD, :]
    utail = ubuf[tl:tl + POOL_PAD, :]
    nps_ref[...] = utail
    vbuf[0:CONV_PAD, :] = vtail
    ubuf[0:POOL_PAD, :] = utail


def _const_spec(shape):
    nd = len(shape)
    return pl.BlockSpec(shape, lambda b, l: (0,) * nd, pipeline_mode=pl.Buffered(1))


def _mixer(x, mod3, boff, hc, hp, start_pos, wts, tl, rc):
    bsz, seq, _ = x.shape
    nl = seq // tl
    (g_mix, w_in, w_dw, b_dw, g_ln, b_ln, w_pw, w_pool, s_pool, w_out, g_ffn, w_r, b_r) = wts
    kern = functools.partial(_mix_kernel, tl=tl, rc=rc, start_pos=start_pos)
    in_specs = [
        pl.BlockSpec((None, tl, D), lambda b, l: (b, l, 0)),
        pl.BlockSpec((None, 1, N_ADA * D), lambda b, l: (b + boff, 0, 0)),
        pl.BlockSpec((None, CONV_PAD, D), lambda b, l: (b, 0, 0)),
        pl.BlockSpec((None, POOL_PAD, D), lambda b, l: (b, 0, 0)),
    ] + [_const_spec(w.shape) for w in wts]
    out_specs = [
        pl.BlockSpec((None, tl, D), lambda b, l: (b, l, 0)),
        pl.BlockSpec((tl * ROW_VREGS, LANES), lambda b, l: (b * nl + l, 0)),
        pl.BlockSpec((None, SUBLANES, tl), lambda b, l: (b, 0, l)),
        pl.BlockSpec((None, CONV_PAD, D), lambda b, l: (b, 0, 0)),
        pl.BlockSpec((None, POOL_PAD, D), lambda b, l: (b, 0, 0)),
    ]
    out_shape = [
        jax.ShapeDtypeStruct((bsz, seq, D), _F32),
        jax.ShapeDtypeStruct((bsz * seq * ROW_VREGS, LANES), _F32),
        jax.ShapeDtypeStruct((bsz, SUBLANES, seq), _F32),
        jax.ShapeDtypeStruct((bsz, CONV_PAD, D), _F32),
        jax.ShapeDtypeStruct((bsz, POOL_PAD, D), _F32),
    ]
    scratch = [
        pltpu.VMEM((tl, D), _BF16),
        pltpu.VMEM((tl + CONV_PAD, D), _F32),
        pltpu.VMEM((tl + POOL_PAD, D), _F32),
        pltpu.VMEM((tl, D), _F32),
        pltpu.VMEM((tl, D), _F32),
        pltpu.VMEM((tl, D), _F32),
        pltpu.VMEM((tl, D), _F32),
        pltpu.VMEM((tl, D), _BF16),
    ]
    return pl.pallas_call(
        kern,
        grid=(bsz, nl),
        in_specs=in_specs,
        out_specs=out_specs,
        out_shape=out_shape,
        scratch_shapes=scratch,
        compiler_params=pltpu.CompilerParams(
            dimension_semantics=("arbitrary", "arbitrary"), vmem_limit_bytes=VMEM_LIMIT),
        name="mixer",
    )(x, mod3, hc, hp, *wts)


def _moe_kernel(nblk_ref, bstart_ref, tok_ref, wt_ref, src_ref, wgu_ref, wdn_ref, out_ref,
                xc_scr, yc_scr, *, s_tok, unroll):
    s = pl.program_id(0)
    e = pl.program_id(1)

    @pl.when(e == 0)
    def _():
        out_ref[...] = jnp.zeros(out_ref.shape, _F32)

    nb = nblk_ref[s * N_EXPERTS + e]
    b0 = bstart_ref[s * N_EXPERTS + e]

    def block(b, carry):
        base = (b0 + b) * MOE_ROWS
        for mi in range(MOE_ROWS):
            t = jnp.minimum(tok_ref[base + mi], s_tok - 1)
            xc_scr[pl.ds(mi, ROW_VREGS, stride=MOE_STRIDE), :] = src_ref[t]
        x = jnp.concatenate(
            [xc_scr[pl.ds(j * MOE_STRIDE, MOE_ROWS), :] for j in range(ROW_VREGS)], axis=1)
        h = _dot(x.astype(_BF16), wgu_ref[...])
        g = h[:, :D_EXPERT]
        act = (g * _sigmoid(g) * h[:, D_EXPERT:]).astype(_BF16)
        y = _dot(act, wdn_ref[...])
        for j in range(ROW_VREGS):
            yc_scr[pl.ds(j * MOE_STRIDE, MOE_ROWS), :] = y[:, j * LANES:(j + 1) * LANES]
        for g0 in range(0, MOE_ROWS, unroll):
            ts = [tok_ref[base + g0 + i] for i in range(unroll)]
            vals = [out_ref[ts[i]] + wt_ref[base + g0 + i]
                    * yc_scr[pl.ds(g0 + i, ROW_VREGS, stride=MOE_STRIDE), :] for i in range(unroll)]
            for i in range(unroll):
                out_ref[ts[i]] = vals[i]
        return carry

    lax.fori_loop(0, nb, block, 0)


def _moe(nblk, bstart, tok, wt, n2t, w_gu, w_dn, s_tok, pad_tok, p_max):
    n_super = n2t.shape[0] // s_tok
    kern = functools.partial(_moe_kernel, s_tok=s_tok, unroll=8)
    grid_spec = pltpu.PrefetchScalarGridSpec(
        num_scalar_prefetch=2,
        grid=(n_super, N_EXPERTS),
        in_specs=[
            pl.BlockSpec((p_max,), lambda s, e, *_: (s,), memory_space=pltpu.SMEM),
            pl.BlockSpec((p_max,), lambda s, e, *_: (s,), memory_space=pltpu.SMEM),
            pl.BlockSpec((s_tok, ROW_VREGS, LANES), lambda s, e, *_: (s, 0, 0),
                         pipeline_mode=pl.Buffered(1)),
            pl.BlockSpec((None, D, D), lambda s, e, *_: (e, 0, 0)),
            pl.BlockSpec((None, D_EXPERT, D), lambda s, e, *_: (e, 0, 0)),
        ],
        out_specs=pl.BlockSpec((None, s_tok + pad_tok, ROW_VREGS, LANES), lambda s, e, *_: (s, 0, 0, 0),
                               pipeline_mode=pl.Buffered(1)),
        scratch_shapes=[
            pltpu.VMEM((ROW_VREGS * MOE_STRIDE, LANES), _F32),
            pltpu.VMEM((ROW_VREGS * MOE_STRIDE, LANES), _F32),
        ],
    )
    return pl.pallas_call(
        kern,
        grid_spec=grid_spec,
        out_shape=jax.ShapeDtypeStruct((n_super, s_tok + pad_tok, ROW_VREGS, LANES), _F32),
        compiler_params=pltpu.CompilerParams(
            dimension_semantics=("arbitrary", "arbitrary"), vmem_limit_bytes=VMEM_LIMIT),
        name="moe",
    )(nblk, bstart, tok, wt, n2t, w_gu, w_dn)


def _dispatch(route, s_tok):
    bsz, _, seq = route.shape
    n_tok = bsz * seq
    n_super = n_tok // s_tok
    r = jnp.transpose(route[:, 0:4, :], (0, 2, 1)).reshape(n_super, s_tok, 4)
    flat_e = r[:, :, 0:2].astype(jnp.int32).reshape(n_super, 2 * s_tok)
    flat_w = r[:, :, 2:4].reshape(n_super, 2 * s_tok)
    order = jnp.argsort(flat_e, axis=1, stable=True)
    st = (order // 2).astype(jnp.int32)
    sw = jnp.take_along_axis(flat_w, order, axis=1)
    counts = jnp.sum(flat_e[:, :, None] == jnp.arange(N_EXPERTS, dtype=jnp.int32), axis=1).astype(jnp.int32)
    starts = jnp.cumsum(counts, axis=1) - counts
    pcounts = (counts + MOE_ROWS - 1) // MOE_ROWS * MOE_ROWS
    pends = jnp.cumsum(pcounts, axis=1)
    pstarts = pends - pcounts
    p_max = (2 * s_tok + N_EXPERTS * (MOE_ROWS - 1) + MOE_ROWS - 1) // MOE_ROWS * MOE_ROWS
    p_alloc = (p_max + 1023) // 1024 * 1024
    pos = jnp.arange(p_alloc, dtype=jnp.int32)
    pe = jnp.minimum(jnp.sum(pos[None, :, None] >= pends[:, None, :], axis=2), N_EXPERTS - 1).astype(jnp.int32)
    off = pos[None, :] - jnp.take_along_axis(pstarts, pe, axis=1)
    valid = off < jnp.take_along_axis(counts, pe, axis=1)
    srcpos = jnp.clip(jnp.take_along_axis(starts, pe, axis=1) + off, 0, 2 * s_tok - 1)
    tok = jnp.where(valid, jnp.take_along_axis(st, srcpos, axis=1), s_tok).astype(jnp.int32)
    wt = jnp.where(valid, jnp.take_along_axis(sw, srcpos, axis=1), 0.0).astype(_F32)
    nblk = (pcounts // MOE_ROWS).reshape(-1)
    bstart = (pstarts // MOE_ROWS).reshape(-1)
    return nblk, bstart, tok.reshape(-1), wt.reshape(-1), p_alloc


def _fin_kernel(x1_ref, f_ref, mod_ref, g_ref, o_ref, *, tlf):
    f = jnp.concatenate(
        [f_ref[pl.ds(j, tlf, stride=ROW_VREGS), :] for j in range(ROW_VREGS)], axis=1)
    x2 = x1_ref[...] + mod_ref[...] * f
    ms = jnp.mean(x2 * x2, axis=-1, keepdims=True)
    o_ref[...] = x2 * lax.rsqrt(ms + EPS) * g_ref[...]


def _final(x1, acc2d, mod3, boff, g_final, s_tok, tlf):
    bsz, seq, _ = x1.shape
    n_tok = bsz * seq
    n_super = n_tok // s_tok
    per = s_tok // tlf
    kern = functools.partial(_fin_kernel, tlf=tlf)
    y = pl.pallas_call(
        kern,
        grid=(n_super, per),
        in_specs=[
            pl.BlockSpec((tlf, D), lambda s, i: (s * per + i, 0)),
            pl.BlockSpec((tlf * ROW_VREGS, LANES), lambda s, i: (s * (per + 1) + i, 0)),
            pl.BlockSpec((None, 1, D), lambda s, i: ((s * s_tok + i * tlf) // seq + boff, 0, N_ADA - 1)),
            pl.BlockSpec((1, D), lambda s, i: (0, 0)),
        ],
        out_specs=pl.BlockSpec((tlf, D), lambda s, i: (s * per + i, 0)),
        out_shape=jax.ShapeDtypeStruct((n_tok, D), _F32),
        name="final",
    )(x1.reshape(n_tok, D), acc2d, mod3, g_final)
    return y.reshape(bsz, seq, D)


def _stream(x, mod3, boff, hc, hp, start_pos, mix_w, w_gu, w_dn, g_final, tl, rc, s_tok, tlf):
    bsz, seq, _ = x.shape
    x1, n2t, route, ncs, nps = _mixer(x, mod3, boff, hc, hp, start_pos, mix_w, tl, rc)
    nblk, bstart, tok, wt, p_alloc = _dispatch(route, s_tok)
    n2t3 = n2t.reshape(bsz * seq, ROW_VREGS, LANES)
    acc = _moe(nblk, bstart, tok, wt, n2t3, w_gu, w_dn, s_tok, tlf, p_alloc)
    acc2d = acc.reshape(-1, LANES)
    y = _final(x1, acc2d, mod3, boff, g_final, s_tok, tlf)
    return y, ncs[None, :, CONV_PAD - (CONV_WIDTH - 1):, :], nps[None, :, 1:, :]


def kernel(x_prompt, x_sample, state_conv, state_pool, c_prompt, c_sample, w_ada, b_ada, g_mix, w_in, w_dw, b_dw, g_ln, b_ln, w_pw, w_pool, s_pool, w_out, g_ffn, w_rg, b_rg, w_re, b_re, w_gate, w_up, w_down, g_final):
    bp, lp, _ = x_prompt.shape
    bs, ls, _ = x_sample.shape
    past_len = 1024

    c_all = jnp.concatenate([c_prompt, c_sample], axis=0)
    mod = _ada(c_all, w_ada[0], b_ada[0][None, :])
    mod3 = mod[:, None, :]

    w_r = jnp.concatenate(
        [w_rg[0], w_re[0], jnp.zeros((D, N_ROUTE - N_GROUPS - N_EXPERTS), _F32)], axis=1).T.astype(_BF16)
    b_r = jnp.concatenate(
        [b_rg[0], b_re[0], jnp.zeros((N_ROUTE - N_GROUPS - N_EXPERTS,), _F32)])[:, None]
    w_dw_p = jnp.concatenate([w_dw[0], jnp.zeros((1, D), _F32)], axis=0)
    mix_w = (g_mix[0][None, :], w_in[0].astype(_BF16), w_dw_p, b_dw[0][None, :], g_ln[0][None, :],
             b_ln[0][None, :], w_pw[0].astype(_BF16), w_pool[0].astype(_BF16), s_pool[0][None, :],
             w_out[0].astype(_BF16), g_ffn[0][None, :], w_r, b_r)
    w_gu = jnp.concatenate([w_gate[0], w_up[0]], axis=2).astype(_BF16)
    w_dn = w_down[0].astype(_BF16)
    g_fin = g_final[None, :]

    hc_p = jnp.zeros((bp, CONV_PAD, D), _F32)
    hp_p = jnp.zeros((bp, POOL_PAD, D), _F32)
    hc_s = jnp.pad(state_conv[0], ((0, 0), (CONV_PAD - (CONV_WIDTH - 1), 0), (0, 0)))
    hp_s = jnp.pad(state_pool[0], ((0, 0), (1, 0), (0, 0)))

    y_p, ncp, npp = _stream(x_prompt, mod3, 0, hc_p, hp_p, 0, mix_w, w_gu, w_dn, g_fin,
                            tl=256, rc=64, s_tok=4096, tlf=256)
    y_s, ncs, nps = _stream(x_sample, mod3, bp, hc_s, hp_s, past_len, mix_w, w_gu, w_dn, g_fin,
                            tl=ls, rc=ls, s_tok=bs * ls, tlf=ls)
    return (y_p, y_s, ncp, npp, ncs, nps)
```

```python
import functools

import jax
import jax.numpy as jnp
from jax import lax
from jax.experimental import pallas as pl
from jax.experimental.pallas import tpu as pltpu

D = 1024
N_IN = 5
N_ADA = 6
CONV_WIDTH = 31
CONV_PAD = 32
POOL_PAD = 16
POOL_WINDOWS = (2, 4, 8, 16)
POOL_GROUP_W = D // 4
N_GROUPS = 4
EPG = 8
N_EXPERTS = N_GROUPS * EPG
N_ROUTE = 40
D_EXPERT = D // 2
EPS = 1e-6
LANES = 128
SUBLANES = 8
ROW_VREGS = D // LANES
MOE_ROWS = 128
MOE_STRIDE = MOE_ROWS + 8
MOE_LEAD = 1024
VMEM_LIMIT = 56 * 1024 * 1024

_F32 = jnp.float32
_BF16 = jnp.bfloat16


def _dot(a, b):
    return jnp.dot(a, b, preferred_element_type=_F32)


def _sigmoid(x):
    return jax.nn.sigmoid(x)


def _round_bf16(x):
    return x.astype(_BF16).astype(_F32)


def _ada_kernel(c_ref, w_ref, b_ref, o_ref):
    c = c_ref[...]
    s = (c * _sigmoid(c)).astype(_BF16)
    o_ref[...] = _dot(s, w_ref[...].astype(_BF16)) + b_ref[...]


def _ada(c_all, w_ada, b_ada):
    nb = c_all.shape[0]
    return pl.pallas_call(
        _ada_kernel,
        grid=(N_ADA,),
        in_specs=[
            pl.BlockSpec((nb, D), lambda j: (0, 0)),
            pl.BlockSpec((D, D), lambda j: (0, j)),
            pl.BlockSpec((1, D), lambda j: (0, j)),
        ],
        out_specs=pl.BlockSpec((nb, D), lambda j: (0, j)),
        out_shape=jax.ShapeDtypeStruct((nb, N_ADA * D), _F32),
        name="ada",
    )(c_all, w_ada, b_ada)


def _mix_kernel(x_ref, mod_ref, hc_ref, hp_ref, gmix_ref, win_ref, wdw_ref, bdw_ref, gln_ref,
                bln_ref, wpw_ref, wpool_ref, spool_ref, wout_ref, gffn_ref, wr_ref, br_ref,
                x1_ref, n2t_ref, route_ref, ncs_ref, nps_ref,
                nb_scr, vbuf, ubuf, t1, t2, t3, t4, ab_scr, *, tl, rc, start_pos):
    l = pl.program_id(1)
    nchunk = tl // rc

    def chunks(body):
        def step(i, carry):
            body(pl.multiple_of(i * rc, rc))
            return carry
        lax.fori_loop(0, nchunk, step, 0)

    @pl.when(l == 0)
    def _():
        vbuf[0:CONV_PAD, :] = _round_bf16(hc_ref[...])
        ubuf[0:POOL_PAD, :] = hp_ref[...]

    sh1 = mod_ref[:, 0 * D:1 * D]
    sc1 = mod_ref[:, 1 * D:2 * D]
    gt1 = mod_ref[:, 2 * D:3 * D]
    sh2 = mod_ref[:, 3 * D:4 * D]
    sc2 = mod_ref[:, 4 * D:5 * D]

    def norm1(r):
        xc = x_ref[pl.ds(r, rc), :]
        ms = jnp.mean(xc * xc, axis=-1, keepdims=True)
        n = xc * lax.rsqrt(ms + EPS) * gmix_ref[...]
        nb_scr[pl.ds(r, rc), :] = (n * (1.0 + sc1) + sh1).astype(_BF16)
    chunks(norm1)

    t1[...] = _dot(nb_scr[...], win_ref[:, 0 * D:1 * D])
    t2[...] = _dot(nb_scr[...], win_ref[:, 1 * D:2 * D])

    def glu(r):
        v = t1[pl.ds(r, rc), :] * _sigmoid(t2[pl.ds(r, rc), :])
        vbuf[pl.ds(CONV_PAD + r, rc), :] = _round_bf16(v)
    chunks(glu)
    ncs_ref[...] = t1[tl - CONV_PAD:tl, :] * _sigmoid(t2[tl - CONV_PAD:tl, :])

    def conv(r):
        for c in range(ROW_VREGS):
            cs = slice(c * LANES, (c + 1) * LANES)
            win = vbuf[pl.ds(r, rc + CONV_PAD), cs]
            acc = bdw_ref[:, cs] + wdw_ref[CONV_WIDTH - 1:CONV_WIDTH, cs] * win[CONV_PAD:CONV_PAD + rc, :]
            for rr in range(SUBLANES):
                shifted = win[rr:rr + rc + CONV_PAD - SUBLANES, :]
                for q in range(CONV_PAD // SUBLANES):
                    k = q * SUBLANES + rr - (CONV_PAD - CONV_WIDTH + 1)
                    if 0 <= k < CONV_WIDTH:
                        acc = acc + wdw_ref[k:k + 1, cs] * shifted[q * SUBLANES:q * SUBLANES + rc, :]
            t1[pl.ds(r, rc), cs] = acc
        cv = t1[pl.ds(r, rc), :]
        mu = jnp.mean(cv, axis=-1, keepdims=True)
        xc = cv - mu
        a = xc * lax.rsqrt(jnp.mean(xc * xc, axis=-1, keepdims=True) + EPS) * gln_ref[...] + bln_ref[...]
        ab_scr[pl.ds(r, rc), :] = (a * _sigmoid(a)).astype(_BF16)
    chunks(conv)

    t1[...] = _dot(ab_scr[...], wpw_ref[...])

    ubuf[POOL_PAD:POOL_PAD + tl, :] = _dot(nb_scr[...], win_ref[:, 2 * D:3 * D])

    def pool(r):
        pos = start_pos + l * tl + r + lax.broadcasted_iota(jnp.int32, (rc, 1), 0)
        for g, w in enumerate(POOL_WINDOWS):
            gs = slice(g * POOL_GROUP_W, (g + 1) * POOL_GROUP_W)
            win = ubuf[pl.ds(r, rc + POOL_PAD), gs]
            u = win[POOL_PAD:POOL_PAD + rc, :]
            s = win
            shift = 1
            while shift < w:
                s = s[shift:, :] + s[:s.shape[0] - shift, :]
                shift *= 2
            off = POOL_PAD - (w - 1)
            sw = s[off:off + rc, :]
            cnt = jnp.minimum(pos + 1, w).astype(_F32)
            ab_scr[pl.ds(r, rc), gs] = (sw * (1.0 / cnt) - u).astype(_BF16)
    chunks(pool)

    for g in range(N_GROUPS):
        gs = slice(g * POOL_GROUP_W, (g + 1) * POOL_GROUP_W)
        t2[:, gs] = _dot(ab_scr[:, gs], wpool_ref[g]) * spool_ref[:, gs]

    t3[...] = _dot(nb_scr[...], win_ref[:, 3 * D:4 * D])
    t4[...] = _dot(nb_scr[...], win_ref[:, 4 * D:5 * D])

    def mix(r):
        rs = pl.ds(r, rc)
        m = _sigmoid(t3[rs, :]) * t1[rs, :] + _sigmoid(t4[rs, :]) * t2[rs, :]
        ab_scr[rs, :] = m.astype(_BF16)
    chunks(mix)

    t1[...] = _dot(ab_scr[...], wout_ref[...])

    def resid(r):
        rs = pl.ds(r, rc)
        x1 = x_ref[rs, :] + gt1 * t1[rs, :]
        x1_ref[rs, :] = x1
        ms = jnp.mean(x1 * x1, axis=-1, keepdims=True)
        n2 = x1 * lax.rsqrt(ms + EPS) * gffn_ref[...] * (1.0 + sc2) + sh2
        t2[rs, :] = n2
        nb_scr[rs, :] = n2.astype(_BF16)
    chunks(resid)

    for j in range(ROW_VREGS):
        n2t_ref[pl.ds(j, tl, stride=ROW_VREGS), :] = t2[:, j * LANES:(j + 1) * LANES]

    lg = lax.dot_general(wr_ref[...], nb_scr[...], (((1,), (1,)), ((), ())),
                         preferred_element_type=_F32) + br_ref[...]
    lgg = lg[0:N_GROUPS, :]
    mg = jnp.max(lgg, axis=0, keepdims=True)
    p_g = 1.0 / jnp.sum(jnp.exp(lgg - mg), axis=0, keepdims=True)
    gi = lax.broadcasted_iota(jnp.int32, lgg.shape, 0)
    g_idx = jnp.min(jnp.where(lgg == mg, gi, N_GROUPS), axis=0, keepdims=True)
    sel = jnp.zeros((EPG, tl), _F32)
    for g in range(N_GROUPS):
        sel = jnp.where(g_idx == g, lg[N_GROUPS + g * EPG:N_GROUPS + (g + 1) * EPG, :], sel)
    ms = jnp.max(sel, axis=0, keepdims=True)
    es = jnp.exp(sel - ms)
    pe = es / jnp.sum(es, axis=0, keepdims=True)
    ei = lax.broadcasted_iota(jnp.int32, pe.shape, 0)
    v1 = jnp.max(pe, axis=0, keepdims=True)
    i1 = jnp.min(jnp.where(pe == v1, ei, EPG), axis=0, keepdims=True)
    pe2 = jnp.where(ei == i1, -1.0, pe)
    v2 = jnp.max(pe2, axis=0, keepdims=True)
    i2 = jnp.min(jnp.where(pe2 == v2, ei, EPG), axis=0, keepdims=True)
    scale = p_g / (v1 + v2)
    route_ref[0:1, :] = (g_idx * EPG + i1).astype(_F32)
    route_ref[1:2, :] = (g_idx * EPG + i2).astype(_F32)
    route_ref[2:3, :] = v1 * scale
    route_ref[3:4, :] = v2 * scale
    route_ref[4:8, :] = jnp.zeros((4, tl), _F32)

    vtail = vbuf[tl:tl + CONV_PAD, :]
    utail = ubuf[tl:tl + POOL_PAD, :]
    nps_ref[...] = utail
    vbuf[0:CONV_PAD, :] = vtail
    ubuf[0:POOL_PAD, :] = utail


def _const_spec(shape):
    nd = len(shape)
    return pl.BlockSpec(shape, lambda b, l: (0,) * nd, pipeline_mode=pl.Buffered(1))


def _mixer(x, mod3, boff, hc, hp, start_pos, wts, tl, rc):
    bsz, seq, _ = x.shape
    nl = seq // tl
    (g_mix, w_in, w_dw, b_dw, g_ln, b_ln, w_pw, w_pool, s_pool, w_out, g_ffn, w_r, b_r) = wts
    kern = functools.partial(_mix_kernel, tl=tl, rc=rc, start_pos=start_pos)
    in_specs = [
        pl.BlockSpec((None, tl, D), lambda b, l: (b, l, 0)),
        pl.BlockSpec((None, 1, N_ADA * D), lambda b, l: (b + boff, 0, 0)),
        pl.BlockSpec((None, CONV_PAD, D), lambda b, l: (b, 0, 0)),
        pl.BlockSpec((None, POOL_PAD, D), lambda b, l: (b, 0, 0)),
    ] + [_const_spec(w.shape) for w in wts]
    out_specs = [
        pl.BlockSpec((None, tl, D), lambda b, l: (b, l, 0)),
        pl.BlockSpec((tl * ROW_VREGS, LANES), lambda b, l: (b * nl + l, 0)),
        pl.BlockSpec((None, SUBLANES, tl), lambda b, l: (b, 0, l)),
        pl.BlockSpec((None, CONV_PAD, D), lambda b, l: (b, 0, 0)),
        pl.BlockSpec((None, POOL_PAD, D), lambda b, l: (b, 0, 0)),
    ]
    out_shape = [
        jax.ShapeDtypeStruct((bsz, seq, D), _F32),
        jax.ShapeDtypeStruct((bsz * seq * ROW_VREGS, LANES), _F32),
        jax.ShapeDtypeStruct((bsz, SUBLANES, seq), _F32),
        jax.ShapeDtypeStruct((bsz, CONV_PAD, D), _F32),
        jax.ShapeDtypeStruct((bsz, POOL_PAD, D), _F32),
    ]
    scratch = [
        pltpu.VMEM((tl, D), _BF16),
        pltpu.VMEM((tl + CONV_PAD, D), _F32),
        pltpu.VMEM((tl + POOL_PAD, D), _F32),
        pltpu.VMEM((tl, D), _F32),
        pltpu.VMEM((tl, D), _F32),
        pltpu.VMEM((tl, D), _F32),
        pltpu.VMEM((tl, D), _F32),
        pltpu.VMEM((tl, D), _BF16),
    ]
    return pl.pallas_call(
        kern,
        grid=(bsz, nl),
        in_specs=in_specs,
        out_specs=out_specs,
        out_shape=out_shape,
        scratch_shapes=scratch,
        compiler_params=pltpu.CompilerParams(
            dimension_semantics=("arbitrary", "arbitrary"), vmem_limit_bytes=VMEM_LIMIT),
        name="mixer",
    )(x, mod3, hc, hp, *wts)


def _moe_kernel(nblk_ref, bstart_ref, tok_ref, wt_ref, src_ref, wgu_ref, wdn_ref, out_ref,
                xc_scr, yc_scr, xb_scr, *, s_tok, unroll):
    s = pl.program_id(0)
    e = pl.program_id(1)

    def gather(blk):
        base = MOE_LEAD + blk * MOE_ROWS
        for mi in range(MOE_ROWS):
            t = jnp.minimum(tok_ref[base + mi], s_tok - 1)
            xc_scr[pl.ds(mi, ROW_VREGS, stride=MOE_STRIDE), :] = src_ref[t]

    def scatter(blk):
        base = MOE_LEAD + blk * MOE_ROWS
        for g0 in range(0, MOE_ROWS, unroll):
            ts = [tok_ref[base + g0 + i] for i in range(unroll)]
            vals = [out_ref[ts[i]] + wt_ref[base + g0 + i]
                    * yc_scr[pl.ds(g0 + i, ROW_VREGS, stride=MOE_STRIDE), :] for i in range(unroll)]
            for i in range(unroll):
                out_ref[ts[i]] = vals[i]

    @pl.when(e == 0)
    def _():
        out_ref[...] = jnp.zeros(out_ref.shape, _F32)
        yc_scr[...] = jnp.zeros(yc_scr.shape, _F32)
        gather(0)

    nb = nblk_ref[s * N_EXPERTS + e]
    b0 = bstart_ref[s * N_EXPERTS + e]

    def block(b, carry):
        blk = b0 + b
        xb_scr[...] = jnp.concatenate(
            [xc_scr[pl.ds(j * MOE_STRIDE, MOE_ROWS), :] for j in range(ROW_VREGS)], axis=1).astype(_BF16)
        gather(blk + 1)
        scatter(blk - 1)
        h = _dot(xb_scr[...], wgu_ref[...])
        g = h[:, :D_EXPERT]
        act = (g * _sigmoid(g) * h[:, D_EXPERT:]).astype(_BF16)
        y = _dot(act, wdn_ref[...])
        for j in range(ROW_VREGS):
            yc_scr[pl.ds(j * MOE_STRIDE, MOE_ROWS), :] = y[:, j * LANES:(j + 1) * LANES]
        return carry

    lax.fori_loop(0, nb, block, 0)

    @pl.when(e == N_EXPERTS - 1)
    def _():
        scatter(b0 + nb - 1)


def _moe(nblk, bstart, tok, wt, n2t, w_gu, w_dn, s_tok, pad_tok, p_max):
    n_super = n2t.shape[0] // s_tok
    kern = functools.partial(_moe_kernel, s_tok=s_tok, unroll=8)
    grid_spec = pltpu.PrefetchScalarGridSpec(
        num_scalar_prefetch=2,
        grid=(n_super, N_EXPERTS),
        in_specs=[
            pl.BlockSpec((p_max,), lambda s, e, *_: (s,), memory_space=pltpu.SMEM),
            pl.BlockSpec((p_max,), lambda s, e, *_: (s,), memory_space=pltpu.SMEM),
            pl.BlockSpec((s_tok, ROW_VREGS, LANES), lambda s, e, *_: (s, 0, 0),
                         pipeline_mode=pl.Buffered(1)),
            pl.BlockSpec((None, D, D), lambda s, e, *_: (e, 0, 0)),
            pl.BlockSpec((None, D_EXPERT, D), lambda s, e, *_: (e, 0, 0)),
        ],
        out_specs=pl.BlockSpec((None, s_tok + pad_tok, ROW_VREGS, LANES), lambda s, e, *_: (s, 0, 0, 0),
                               pipeline_mode=pl.Buffered(1)),
        scratch_shapes=[
            pltpu.VMEM((ROW_VREGS * MOE_STRIDE, LANES), _F32),
            pltpu.VMEM((ROW_VREGS * MOE_STRIDE, LANES), _F32),
            pltpu.VMEM((MOE_ROWS, D), _BF16),
        ],
    )
    return pl.pallas_call(
        kern,
        grid_spec=grid_spec,
        out_shape=jax.ShapeDtypeStruct((n_super, s_tok + pad_tok, ROW_VREGS, LANES), _F32),
        compiler_params=pltpu.CompilerParams(
            dimension_semantics=("arbitrary", "arbitrary"), vmem_limit_bytes=VMEM_LIMIT),
        name="moe",
    )(nblk, bstart, tok, wt, n2t, w_gu, w_dn)


def _dispatch(route, s_tok):
    bsz, _, seq = route.shape
    n_tok = bsz * seq
    n_super = n_tok // s_tok
    r = jnp.transpose(route[:, 0:4, :], (1, 0, 2)).reshape(4, n_super, s_tok)
    e_real = jnp.concatenate([r[0], r[1]], axis=1).astype(jnp.int32)
    w_real = jnp.concatenate([r[2], r[3]], axis=1)
    t_real = jnp.broadcast_to(jnp.tile(jnp.arange(s_tok, dtype=jnp.int32), 2)[None, :], e_real.shape)
    experts = jnp.arange(N_EXPERTS, dtype=jnp.int32)
    counts = jnp.sum(e_real[:, :, None] == experts, axis=1).astype(jnp.int32)
    n_pad = (-counts) % MOE_ROWS
    cand = jnp.arange(MOE_ROWS - 1, dtype=jnp.int32)
    key_pad = jnp.where(cand[None, None, :] < n_pad[:, :, None], experts[None, :, None], N_EXPERTS)
    key_pad = key_pad.reshape(n_super, N_EXPERTS * (MOE_ROWS - 1)).astype(jnp.int32)
    n_sort = 2 * s_tok + N_EXPERTS * (MOE_ROWS - 1)
    p_alloc = (n_sort + MOE_ROWS + 1023) // 1024 * 1024
    keys = jnp.concatenate([e_real, key_pad], axis=1)
    toks = jnp.concatenate([t_real, jnp.full((n_super, n_sort - 2 * s_tok), s_tok, jnp.int32)], axis=1)
    wts = jnp.concatenate([w_real, jnp.zeros((n_super, n_sort - 2 * s_tok), _F32)], axis=1)
    _, tok, wt = lax.sort((keys, toks, wts), dimension=1, num_keys=1, is_stable=True)
    tok = jnp.pad(tok, ((0, 0), (MOE_LEAD, p_alloc - n_sort)), constant_values=s_tok)
    wt = jnp.pad(wt, ((0, 0), (MOE_LEAD, p_alloc - n_sort)))
    p_alloc += MOE_LEAD
    pcounts = counts + n_pad
    pstarts = jnp.cumsum(pcounts, axis=1) - pcounts
    nblk = (pcounts // MOE_ROWS).reshape(-1)
    bstart = (pstarts // MOE_ROWS).reshape(-1)
    return nblk, bstart, tok.reshape(-1), wt.reshape(-1), p_alloc


def _fin_kernel(x1_ref, f_ref, mod_ref, g_ref, o_ref, *, tlf):
    f = jnp.concatenate(
        [f_ref[pl.ds(j, tlf, stride=ROW_VREGS), :] for j in range(ROW_VREGS)], axis=1)
    x2 = x1_ref[...] + mod_ref[...] * f
    ms = jnp.mean(x2 * x2, axis=-1, keepdims=True)
    o_ref[...] = x2 * lax.rsqrt(ms + EPS) * g_ref[...]


def _final(x1, acc2d, mod3, boff, g_final, s_tok, tlf):
    bsz, seq, _ = x1.shape
    n_tok = bsz * seq
    n_super = n_tok // s_tok
    per = s_tok // tlf
    kern = functools.partial(_fin_kernel, tlf=tlf)
    y = pl.pallas_call(
        kern,
        grid=(n_super, per),
        in_specs=[
            pl.BlockSpec((tlf, D), lambda s, i: (s * per + i, 0)),
            pl.BlockSpec((tlf * ROW_VREGS, LANES), lambda s, i: (s * (per + 1) + i, 0)),
            pl.BlockSpec((None, 1, D), lambda s, i: ((s * s_tok + i * tlf) // seq + boff, 0, N_ADA - 1)),
            pl.BlockSpec((1, D), lambda s, i: (0, 0)),
        ],
        out_specs=pl.BlockSpec((tlf, D), lambda s, i: (s * per + i, 0)),
        out_shape=jax.ShapeDtypeStruct((n_tok, D), _F32),
        name="final",
    )(x1.reshape(n_tok, D), acc2d, mod3, g_final)
    return y.reshape(bsz, seq, D)


def _stream(x, mod3, boff, hc, hp, start_pos, mix_w, w_gu, w_dn, g_final, tl, rc, s_tok, tlf):
    bsz, seq, _ = x.shape
    x1, n2t, route, ncs, nps = _mixer(x, mod3, boff, hc, hp, start_pos, mix_w, tl, rc)
    nblk, bstart, tok, wt, p_alloc = _dispatch(route, s_tok)
    n2t3 = n2t.reshape(bsz * seq, ROW_VREGS, LANES)
    acc = _moe(nblk, bstart, tok, wt, n2t3, w_gu, w_dn, s_tok, tlf, p_alloc)
    acc2d = acc.reshape(-1, LANES)
    y = _final(x1, acc2d, mod3, boff, g_final, s_tok, tlf)
    return y, ncs[None, :, CONV_PAD - (CONV_WIDTH - 1):, :], nps[None, :, 1:, :]


def kernel(x_prompt, x_sample, state_conv, state_pool, c_prompt, c_sample, w_ada, b_ada, g_mix, w_in, w_dw, b_dw, g_ln, b_ln, w_pw, w_pool, s_pool, w_out, g_ffn, w_rg, b_rg, w_re, b_re, w_gate, w_up, w_down, g_final):
    bp, lp, _ = x_prompt.shape
    bs, ls, _ = x_sample.shape
    past_len = 1024

    c_all = jnp.concatenate([c_prompt, c_sample], axis=0)
    mod = _ada(c_all, w_ada[0], b_ada[0][None, :])
    mod3 = mod[:, None, :]

    w_r = jnp.concatenate(
        [w_rg[0], w_re[0], jnp.zeros((D, N_ROUTE - N_GROUPS - N_EXPERTS), _F32)], axis=1).T.astype(_BF16)
    b_r = jnp.concatenate(
        [b_rg[0], b_re[0], jnp.zeros((N_ROUTE - N_GROUPS - N_EXPERTS,), _F32)])[:, None]
    w_dw_p = jnp.concatenate([w_dw[0], jnp.zeros((1, D), _F32)], axis=0)
    mix_w = (g_mix[0][None, :], w_in[0].astype(_BF16), w_dw_p, b_dw[0][None, :], g_ln[0][None, :],
             b_ln[0][None, :], w_pw[0].astype(_BF16), w_pool[0].astype(_BF16), s_pool[0][None, :],
             w_out[0].astype(_BF16), g_ffn[0][None, :], w_r, b_r)
    w_gu = jnp.concatenate([w_gate[0], w_up[0]], axis=2).astype(_BF16)
    w_dn = w_down[0].astype(_BF16)
    g_fin = g_final[None, :]

    hc_p = jnp.zeros((bp, CONV_PAD, D), _F32)
    hp_p = jnp.zeros((bp, POOL_PAD, D), _F32)
    hc_s = jnp.pad(state_conv[0], ((0, 0), (CONV_PAD - (CONV_WIDTH - 1), 0), (0, 0)))
    hp_s = jnp.pad(state_pool[0], ((0, 0), (1, 0), (0, 0)))

    y_p, ncp, npp = _stream(x_prompt, mod3, 0, hc_p, hp_p, 0, mix_w, w_gu, w_dn, g_fin,
                            tl=256, rc=64, s_tok=4096, tlf=256)
    y_s, ncs, nps = _stream(x_sample, mod3, bp, hc_s, hp_s, past_len, mix_w, w_gu, w_dn, g_fin,
                            tl=ls, rc=ls, s_tok=bs * ls, tlf=ls)
    return (y_p, y_s, ncp, npp, ncs, nps)
```

```python
import functools

import jax
import jax.numpy as jnp
from jax import lax
from jax.experimental import pallas as pl
from jax.experimental.pallas import tpu as pltpu

D = 1024
N_IN = 5
N_ADA = 6
CONV_WIDTH = 31
CONV_PAD = 32
POOL_PAD = 16
HIST_STRIDE = 2
POOL_WINDOWS = (2, 4, 8, 16)
POOL_GROUP_W = D // 4
N_GROUPS = 4
EPG = 8
N_EXPERTS = N_GROUPS * EPG
N_ROUTE = 40
D_EXPERT = D // 2
EPS = 1e-6
LANES = 128
SUBLANES = 8
ROW_VREGS = D // LANES
MOE_ROWS = 128
MOE_STRIDE = MOE_ROWS + 8
MOE_LEAD = 1024
VMEM_LIMIT = 56 * 1024 * 1024

_F32 = jnp.float32
_BF16 = jnp.bfloat16


def _dot(a, b):
    return jnp.dot(a, b, preferred_element_type=_F32)


def _sigmoid(x):
    return jax.nn.sigmoid(x)


def _round_bf16(x):
    return x.astype(_BF16).astype(_F32)


def _ada_kernel(c_ref, w_ref, b_ref, o_ref):
    c = c_ref[...]
    s = (c * _sigmoid(c)).astype(_BF16)
    o_ref[...] = _dot(s, w_ref[...].astype(_BF16)) + b_ref[...]


def _ada(c_all, w_ada, b_ada):
    nb = c_all.shape[0]
    return pl.pallas_call(
        _ada_kernel,
        grid=(N_ADA,),
        in_specs=[
            pl.BlockSpec((nb, D), lambda j: (0, 0)),
            pl.BlockSpec((D, D), lambda j: (0, j)),
            pl.BlockSpec((1, D), lambda j: (0, j)),
        ],
        out_specs=pl.BlockSpec((nb, D), lambda j: (0, j)),
        out_shape=jax.ShapeDtypeStruct((nb, N_ADA * D), _F32),
        name="ada",
    )(c_all, w_ada, b_ada)


def _mix_kernel(x_ref, mod_ref, hc_ref, hp_ref, gmix_ref, win_ref, w3_ref, wdw_ref, bdw_ref, gln_ref,
                bln_ref, wpw_ref, wpool_ref, spool_ref, wout_ref, gffn_ref, wr_ref, br_ref,
                x1_ref, n2t_ref, route_ref, ncs_ref, nps_ref,
                nb_scr, vbuf, ubuf, t1, t2, z3, ab_scr, pb_scr, *, tl, rc, start_pos):
    l = pl.program_id(1)
    nchunk = tl // rc
    slab = 3 * D // nchunk

    def chunks(body):
        def step(i, carry):
            body(pl.multiple_of(i * rc, rc))
            return carry
        lax.fori_loop(0, nchunk, step, 0)

    def trow(j, n):
        return pl.ds(HIST_STRIDE * j, n, stride=HIST_STRIDE)

    def hist_put(buf, j, val):
        for lc in range(ROW_VREGS):
            buf[lc, trow(j, val.shape[0]), :] = val[:, lc * LANES:(lc + 1) * LANES]

    def hist_get(buf, j, n):
        return jnp.concatenate([buf[lc, trow(j, n), :] for lc in range(ROW_VREGS)], axis=1)

    @pl.when(l == 0)
    def _():
        hist_put(vbuf, 0, _round_bf16(hc_ref[...]))
        hist_put(ubuf, 0, hp_ref[...])

    sh1 = mod_ref[:, 0 * D:1 * D]
    sc1 = mod_ref[:, 1 * D:2 * D]
    gt1 = mod_ref[:, 2 * D:3 * D]
    sh2 = mod_ref[:, 3 * D:4 * D]
    sc2 = mod_ref[:, 4 * D:5 * D]

    def norm1(r):
        xc = x_ref[pl.ds(r, rc), :]
        ms = jnp.mean(xc * xc, axis=-1, keepdims=True)
        n = xc * lax.rsqrt(ms + EPS) * gmix_ref[...]
        nb_scr[pl.ds(r, rc), :] = (n * (1.0 + sc1) + sh1).astype(_BF16)
    chunks(norm1)

    t1[...] = _dot(nb_scr[...], win_ref[:, 0 * D:1 * D])
    t2[...] = _dot(nb_scr[...], win_ref[:, 1 * D:2 * D])

    def glu(r):
        v = t1[pl.ds(r, rc), :] * _sigmoid(t2[pl.ds(r, rc), :])
        hist_put(vbuf, CONV_PAD + r, _round_bf16(v))
    chunks(glu)
    ncs_ref[...] = t1[tl - CONV_PAD:tl, :] * _sigmoid(t2[tl - CONV_PAD:tl, :])

    def z3_cols(rows, col0, width):
        si, off = divmod(col0, slab)
        assert off + width <= slab
        return z3[si, rows, off:off + width]

    def conv(c, carry):
        r = pl.multiple_of(c * rc, rc)
        for lc in range(ROW_VREGS):
            cs = slice(lc * LANES, (lc + 1) * LANES)
            taps = [jnp.broadcast_to(wdw_ref[k:k + 1, cs], (SUBLANES, LANES)) for k in range(CONV_WIDTH)]
            bias = jnp.broadcast_to(bdw_ref[:, cs], (SUBLANES, LANES))
            ngrp = rc // SUBLANES
            accs = [bias] * ngrp
            for o in range((ngrp - 1) * SUBLANES + CONV_WIDTH):
                xw = vbuf[lc, trow(r + o + (CONV_PAD - CONV_WIDTH + 1), SUBLANES), :]
                for i in range(ngrp):
                    k = o - i * SUBLANES
                    if 0 <= k < CONV_WIDTH:
                        accs[i] = accs[i] + taps[k] * xw
            for i in range(ngrp):
                t1[pl.ds(r + i * SUBLANES, SUBLANES), cs] = accs[i]
        cv = t1[pl.ds(r, rc), :]
        mu = jnp.mean(cv, axis=-1, keepdims=True)
        xc = cv - mu
        a = xc * lax.rsqrt(jnp.mean(xc * xc, axis=-1, keepdims=True) + EPS) * gln_ref[...] + bln_ref[...]
        ab_scr[pl.ds(r, rc), :] = (a * _sigmoid(a)).astype(_BF16)
        z3[c] = _dot(nb_scr[...], w3_ref[c])
        return carry
    lax.fori_loop(0, nchunk, conv, 0)

    hist_put(ubuf, POOL_PAD, jnp.concatenate(
        [z3_cols(slice(None), q * LANES, LANES) for q in range(ROW_VREGS)], axis=1))
    t1[...] = _dot(ab_scr[...], wpw_ref[...])

    for c in range(nchunk):
        r = c * rc
        pos = start_pos + l * tl + r + lax.broadcasted_iota(jnp.int32, (rc, 1), 0)
        for lc in range(ROW_VREGS):
            cs = slice(lc * LANES, (lc + 1) * LANES)
            w = POOL_WINDOWS[lc * LANES // POOL_GROUP_W]
            u = z3_cols(slice(r, r + rc), lc * LANES, LANES)
            sw = u
            for i in range(1, w):
                sw = sw + ubuf[lc, trow(POOL_PAD + r - i, rc), :]
            cnt = jnp.minimum(pos + 1, w).astype(_F32)
            pb_scr[r:r + rc, cs] = (sw * (1.0 / cnt) - u).astype(_BF16)

    for g in range(N_GROUPS):
        gs = slice(g * POOL_GROUP_W, (g + 1) * POOL_GROUP_W)
        t2[:, gs] = _dot(pb_scr[:, gs], wpool_ref[g]) * spool_ref[:, gs]

    def mix(r):
        rs = pl.ds(r, rc)
        for q in range(D // POOL_GROUP_W):
            qs = slice(q * POOL_GROUP_W, (q + 1) * POOL_GROUP_W)
            ga = z3_cols(rs, D + q * POOL_GROUP_W, POOL_GROUP_W)
            gb = z3_cols(rs, 2 * D + q * POOL_GROUP_W, POOL_GROUP_W)
            ab_scr[rs, qs] = (_sigmoid(ga) * t1[rs, qs] + _sigmoid(gb) * t2[rs, qs]).astype(_BF16)
    chunks(mix)

    t1[...] = _dot(ab_scr[...], wout_ref[...])

    def resid(r):
        rs = pl.ds(r, rc)
        x1 = x_ref[rs, :] + gt1 * t1[rs, :]
        x1_ref[rs, :] = x1
        ms = jnp.mean(x1 * x1, axis=-1, keepdims=True)
        n2 = x1 * lax.rsqrt(ms + EPS) * gffn_ref[...] * (1.0 + sc2) + sh2
        t2[rs, :] = n2
        nb_scr[rs, :] = n2.astype(_BF16)
    chunks(resid)

    for j in range(ROW_VREGS):
        n2t_ref[pl.ds(j, tl, stride=ROW_VREGS), :] = t2[:, j * LANES:(j + 1) * LANES]

    lg = lax.dot_general(wr_ref[...], nb_scr[...], (((1,), (1,)), ((), ())),
                         preferred_element_type=_F32) + br_ref[...]
    lgg = lg[0:N_GROUPS, :]
    mg = jnp.max(lgg, axis=0, keepdims=True)
    p_g = 1.0 / jnp.sum(jnp.exp(lgg - mg), axis=0, keepdims=True)
    gi = lax.broadcasted_iota(jnp.int32, lgg.shape, 0)
    g_idx = jnp.min(jnp.where(lgg == mg, gi, N_GROUPS), axis=0, keepdims=True)
    sel = jnp.zeros((EPG, tl), _F32)
    for g in range(N_GROUPS):
        sel = jnp.where(g_idx == g, lg[N_GROUPS + g * EPG:N_GROUPS + (g + 1) * EPG, :], sel)
    ms = jnp.max(sel, axis=0, keepdims=True)
    es = jnp.exp(sel - ms)
    pe = es / jnp.sum(es, axis=0, keepdims=True)
    ei = lax.broadcasted_iota(jnp.int32, pe.shape, 0)
    v1 = jnp.max(pe, axis=0, keepdims=True)
    i1 = jnp.min(jnp.where(pe == v1, ei, EPG), axis=0, keepdims=True)
    pe2 = jnp.where(ei == i1, -1.0, pe)
    v2 = jnp.max(pe2, axis=0, keepdims=True)
    i2 = jnp.min(jnp.where(pe2 == v2, ei, EPG), axis=0, keepdims=True)
    scale = p_g / (v1 + v2)
    route_ref[0:1, :] = (g_idx * EPG + i1).astype(_F32)
    route_ref[1:2, :] = (g_idx * EPG + i2).astype(_F32)
    route_ref[2:3, :] = v1 * scale
    route_ref[3:4, :] = v2 * scale
    route_ref[4:8, :] = jnp.zeros((4, tl), _F32)

    utail = hist_get(ubuf, tl, POOL_PAD)
    nps_ref[...] = utail
    hist_put(vbuf, 0, hist_get(vbuf, tl, CONV_PAD))
    hist_put(ubuf, 0, utail)


def _const_spec(shape):
    nd = len(shape)
    return pl.BlockSpec(shape, lambda b, l: (0,) * nd, pipeline_mode=pl.Buffered(1))


def _mixer(x, mod3, boff, hc, hp, start_pos, wts, tl, rc):
    bsz, seq, _ = x.shape
    nl = seq // tl
    nchunk = tl // rc
    slab = 3 * D // nchunk
    w_in = wts[1]
    w3 = jnp.transpose(w_in[:, 2 * D:].reshape(D, nchunk, slab), (1, 0, 2))
    wts = (wts[0], w_in[:, :2 * D], w3) + tuple(wts[2:])
    kern = functools.partial(_mix_kernel, tl=tl, rc=rc, start_pos=start_pos)
    in_specs = [
        pl.BlockSpec((None, tl, D), lambda b, l: (b, l, 0)),
        pl.BlockSpec((None, 1, N_ADA * D), lambda b, l: (b + boff, 0, 0)),
        pl.BlockSpec((None, CONV_PAD, D), lambda b, l: (b, 0, 0)),
        pl.BlockSpec((None, POOL_PAD, D), lambda b, l: (b, 0, 0)),
    ] + [_const_spec(w.shape) for w in wts]
    out_specs = [
        pl.BlockSpec((None, tl, D), lambda b, l: (b, l, 0)),
        pl.BlockSpec((tl * ROW_VREGS, LANES), lambda b, l: (b * nl + l, 0)),
        pl.BlockSpec((None, SUBLANES, tl), lambda b, l: (b, 0, l)),
        pl.BlockSpec((None, CONV_PAD, D), lambda b, l: (b, 0, 0)),
        pl.BlockSpec((None, POOL_PAD, D), lambda b, l: (b, 0, 0)),
    ]
    out_shape = [
        jax.ShapeDtypeStruct((bsz, seq, D), _F32),
        jax.ShapeDtypeStruct((bsz * seq * ROW_VREGS, LANES), _F32),
        jax.ShapeDtypeStruct((bsz, SUBLANES, seq), _F32),
        jax.ShapeDtypeStruct((bsz, CONV_PAD, D), _F32),
        jax.ShapeDtypeStruct((bsz, POOL_PAD, D), _F32),
    ]
    scratch = [
        pltpu.VMEM((tl, D), _BF16),
        pltpu.VMEM((ROW_VREGS, HIST_STRIDE * (tl + CONV_PAD), LANES), _F32),
        pltpu.VMEM((ROW_VREGS, HIST_STRIDE * (tl + POOL_PAD), LANES), _F32),
        pltpu.VMEM((tl, D), _F32),
        pltpu.VMEM((tl, D), _F32),
        pltpu.VMEM((nchunk, tl, slab), _F32),
        pltpu.VMEM((tl, D), _BF16),
        pltpu.VMEM((tl, D), _BF16),
    ]
    return pl.pallas_call(
        kern,
        grid=(bsz, nl),
        in_specs=in_specs,
        out_specs=out_specs,
        out_shape=out_shape,
        scratch_shapes=scratch,
        compiler_params=pltpu.CompilerParams(
            dimension_semantics=("arbitrary", "arbitrary"), vmem_limit_bytes=VMEM_LIMIT),
        name="mixer",
    )(x, mod3, hc, hp, *wts)


def _moe_kernel(nblk_ref, bstart_ref, tok_ref, wt_ref, src_ref, wgu_ref, wdn_ref, out_ref,
                xc_scr, yc_scr, xb_scr, *, s_tok, unroll):
    s = pl.program_id(0)
    e = pl.program_id(1)

    def gather(blk):
        base = MOE_LEAD + blk * MOE_ROWS
        for mi in range(MOE_ROWS):
            t = jnp.minimum(tok_ref[base + mi], s_tok - 1)
            xc_scr[pl.ds(mi, ROW_VREGS, stride=MOE_STRIDE), :] = src_ref[t]

    def scatter(blk):
        base = MOE_LEAD + blk * MOE_ROWS
        for g0 in range(0, MOE_ROWS, unroll):
            ts = [tok_ref[base + g0 + i] for i in range(unroll)]
            vals = [out_ref[ts[i]] + wt_ref[base + g0 + i]
                    * yc_scr[pl.ds(g0 + i, ROW_VREGS, stride=MOE_STRIDE), :] for i in range(unroll)]
            for i in range(unroll):
                out_ref[ts[i]] = vals[i]

    @pl.when(e == 0)
    def _():
        out_ref[...] = jnp.zeros(out_ref.shape, _F32)
        yc_scr[...] = jnp.zeros(yc_scr.shape, _F32)
        gather(0)

    nb = nblk_ref[s * N_EXPERTS + e]
    b0 = bstart_ref[s * N_EXPERTS + e]

    def block(b, carry):
        blk = b0 + b
        xb_scr[...] = jnp.concatenate(
            [xc_scr[pl.ds(j * MOE_STRIDE, MOE_ROWS), :] for j in range(ROW_VREGS)], axis=1).astype(_BF16)
        gather(blk + 1)
        scatter(blk - 1)
        y = None
        for hs in (slice(0, D_EXPERT // 2), slice(D_EXPERT // 2, D_EXPERT)):
            g = _dot(xb_scr[...], wgu_ref[:, hs])
            u = _dot(xb_scr[...], wgu_ref[:, D_EXPERT + hs.start:D_EXPERT + hs.stop])
            act = (g * _sigmoid(g) * u).astype(_BF16)
            yh = _dot(act, wdn_ref[hs, :])
            y = yh if y is None else y + yh
        for j in range(ROW_VREGS):
            yc_scr[pl.ds(j * MOE_STRIDE, MOE_ROWS), :] = y[:, j * LANES:(j + 1) * LANES]
        return carry

    lax.fori_loop(0, nb, block, 0)

    @pl.when(e == N_EXPERTS - 1)
    def _():
        scatter(b0 + nb - 1)


def _moe(nblk, bstart, tok, wt, n2t, w_gu, w_dn, s_tok, pad_tok, p_max):
    n_super = n2t.shape[0] // s_tok
    kern = functools.partial(_moe_kernel, s_tok=s_tok, unroll=8)
    grid_spec = pltpu.PrefetchScalarGridSpec(
        num_scalar_prefetch=2,
        grid=(n_super, N_EXPERTS),
        in_specs=[
            pl.BlockSpec((p_max,), lambda s, e, *_: (s,), memory_space=pltpu.SMEM),
            pl.BlockSpec((p_max,), lambda s, e, *_: (s,), memory_space=pltpu.SMEM),
            pl.BlockSpec((s_tok, ROW_VREGS, LANES), lambda s, e, *_: (s, 0, 0),
                         pipeline_mode=pl.Buffered(1)),
            pl.BlockSpec((None, D, D), lambda s, e, *_: (e, 0, 0)),
            pl.BlockSpec((None, D_EXPERT, D), lambda s, e, *_: (e, 0, 0)),
        ],
        out_specs=pl.BlockSpec((None, s_tok + pad_tok, ROW_VREGS, LANES), lambda s, e, *_: (s, 0, 0, 0),
                               pipeline_mode=pl.Buffered(1)),
        scratch_shapes=[
            pltpu.VMEM((ROW_VREGS * MOE_STRIDE, LANES), _F32),
            pltpu.VMEM((ROW_VREGS * MOE_STRIDE, LANES), _F32),
            pltpu.VMEM((MOE_ROWS, D), _BF16),
        ],
    )
    return pl.pallas_call(
        kern,
        grid_spec=grid_spec,
        out_shape=jax.ShapeDtypeStruct((n_super, s_tok + pad_tok, ROW_VREGS, LANES), _F32),
        compiler_params=pltpu.CompilerParams(
            dimension_semantics=("arbitrary", "arbitrary"), vmem_limit_bytes=VMEM_LIMIT),
        name="moe",
    )(nblk, bstart, tok, wt, n2t, w_gu, w_dn)


def _dispatch(route, s_tok):
    bsz, _, seq = route.shape
    n_tok = bsz * seq
    n_super = n_tok // s_tok
    r = jnp.transpose(route[:, 0:4, :], (1, 0, 2)).reshape(4, n_super, s_tok)
    e_real = jnp.concatenate([r[0], r[1]], axis=1).astype(jnp.int32)
    w_real = jnp.concatenate([r[2], r[3]], axis=1)
    t_real = jnp.broadcast_to(jnp.tile(jnp.arange(s_tok, dtype=jnp.int32), 2)[None, :], e_real.shape)
    experts = jnp.arange(N_EXPERTS, dtype=jnp.int32)
    counts = jnp.sum(e_real[:, :, None] == experts, axis=1).astype(jnp.int32)
    n_pad = (-counts) % MOE_ROWS
    cand = jnp.arange(MOE_ROWS - 1, dtype=jnp.int32)
    key_pad = jnp.where(cand[None, None, :] < n_pad[:, :, None], experts[None, :, None], N_EXPERTS)
    key_pad = key_pad.reshape(n_super, N_EXPERTS * (MOE_ROWS - 1)).astype(jnp.int32)
    n_sort = 2 * s_tok + N_EXPERTS * (MOE_ROWS - 1)
    p_alloc = (n_sort + MOE_ROWS + 1023) // 1024 * 1024
    keys = jnp.concatenate([e_real, key_pad], axis=1)
    toks = jnp.concatenate([t_real, jnp.full((n_super, n_sort - 2 * s_tok), s_tok, jnp.int32)], axis=1)
    wts = jnp.concatenate([w_real, jnp.zeros((n_super, n_sort - 2 * s_tok), _F32)], axis=1)
    _, tok, wt = lax.sort((keys, toks, wts), dimension=1, num_keys=1, is_stable=True)
    tok = jnp.pad(tok, ((0, 0), (MOE_LEAD, p_alloc - n_sort)), constant_values=s_tok)
    wt = jnp.pad(wt, ((0, 0), (MOE_LEAD, p_alloc - n_sort)))
    p_alloc += MOE_LEAD
    pcounts = counts + n_pad
    pstarts = jnp.cumsum(pcounts, axis=1) - pcounts
    nblk = (pcounts // MOE_ROWS).reshape(-1)
    bstart = (pstarts // MOE_ROWS).reshape(-1)
    return nblk, bstart, tok.reshape(-1), wt.reshape(-1), p_alloc


def _fin_kernel(x1_ref, f_ref, mod_ref, g_ref, o_ref, *, tlf):
    f = jnp.concatenate(
        [f_ref[pl.ds(j, tlf, stride=ROW_VREGS), :] for j in range(ROW_VREGS)], axis=1)
    x2 = x1_ref[...] + mod_ref[...] * f
    ms = jnp.mean(x2 * x2, axis=-1, keepdims=True)
    o_ref[...] = x2 * lax.rsqrt(ms + EPS) * g_ref[...]


def _final(x1, acc2d, mod3, boff, g_final, s_tok, tlf):
    bsz, seq, _ = x1.shape
    n_tok = bsz * seq
    n_super = n_tok // s_tok
    per = s_tok // tlf
    kern = functools.partial(_fin_kernel, tlf=tlf)
    y = pl.pallas_call(
        kern,
        grid=(n_super, per),
        in_specs=[
            pl.BlockSpec((tlf, D), lambda s, i: (s * per + i, 0)),
            pl.BlockSpec((tlf * ROW_VREGS, LANES), lambda s, i: (s * (per + 1) + i, 0)),
            pl.BlockSpec((None, 1, D), lambda s, i: ((s * s_tok + i * tlf) // seq + boff, 0, N_ADA - 1)),
            pl.BlockSpec((1, D), lambda s, i: (0, 0)),
        ],
        out_specs=pl.BlockSpec((tlf, D), lambda s, i: (s * per + i, 0)),
        out_shape=jax.ShapeDtypeStruct((n_tok, D), _F32),
        name="final",
    )(x1.reshape(n_tok, D), acc2d, mod3, g_final)
    return y.reshape(bsz, seq, D)


def _stream(x, mod3, boff, hc, hp, start_pos, mix_w, w_gu, w_dn, g_final, tl, rc, s_tok, tlf):
    bsz, seq, _ = x.shape
    x1, n2t, route, ncs, nps = _mixer(x, mod3, boff, hc, hp, start_pos, mix_w, tl, rc)
    nblk, bstart, tok, wt, p_alloc = _dispatch(route, s_tok)
    n2t3 = n2t.reshape(bsz * seq, ROW_VREGS, LANES)
    acc = _moe(nblk, bstart, tok, wt, n2t3, w_gu, w_dn, s_tok, tlf, p_alloc)
    acc2d = acc.reshape(-1, LANES)
    y = _final(x1, acc2d, mod3, boff, g_final, s_tok, tlf)
    return y, ncs[None, :, CONV_PAD - (CONV_WIDTH - 1):, :], nps[None, :, 1:, :]


def kernel(x_prompt, x_sample, state_conv, state_pool, c_prompt, c_sample, w_ada, b_ada, g_mix, w_in, w_dw, b_dw, g_ln, b_ln, w_pw, w_pool, s_pool, w_out, g_ffn, w_rg, b_rg, w_re, b_re, w_gate, w_up, w_down, g_final):
    bp, lp, _ = x_prompt.shape
    bs, ls, _ = x_sample.shape
    past_len = 1024

    c_all = jnp.concatenate([c_prompt, c_sample], axis=0)
    mod = _ada(c_all, w_ada[0], b_ada[0][None, :])
    mod3 = mod[:, None, :]

    w_r = jnp.concatenate(
        [w_rg[0], w_re[0], jnp.zeros((D, N_ROUTE - N_GROUPS - N_EXPERTS), _F32)], axis=1).T.astype(_BF16)
    b_r = jnp.concatenate(
        [b_rg[0], b_re[0], jnp.zeros((N_ROUTE - N_GROUPS - N_EXPERTS,), _F32)])[:, None]
    w_dw_p = jnp.concatenate([w_dw[0], jnp.zeros((1, D), _F32)], axis=0)
    mix_w = (g_mix[0][None, :], w_in[0].astype(_BF16), w_dw_p, b_dw[0][None, :], g_ln[0][None, :],
             b_ln[0][None, :], w_pw[0].astype(_BF16), w_pool[0].astype(_BF16), s_pool[0][None, :],
             w_out[0].astype(_BF16), g_ffn[0][None, :], w_r, b_r)
    w_gu = jnp.concatenate([w_gate[0], w_up[0]], axis=2).astype(_BF16)
    w_dn = w_down[0].astype(_BF16)
    g_fin = g_final[None, :]

    hc_p = jnp.zeros((bp, CONV_PAD, D), _F32)
    hp_p = jnp.zeros((bp, POOL_PAD, D), _F32)
    hc_s = jnp.pad(state_conv[0], ((0, 0), (CONV_PAD - (CONV_WIDTH - 1), 0), (0, 0)))
    hp_s = jnp.pad(state_pool[0], ((0, 0), (1, 0), (0, 0)))

    y_p, ncp, npp = _stream(x_prompt, mod3, 0, hc_p, hp_p, 0, mix_w, w_gu, w_dn, g_fin,
                            tl=256, rc=64, s_tok=4096, tlf=256)
    y_s, ncs, nps = _stream(x_sample, mod3, bp, hc_s, hp_s, past_len, mix_w, w_gu, w_dn, g_fin,
                            tl=ls, rc=ls, s_tok=bs * ls, tlf=ls)
    return (y_p, y_s, ncp, npp, ncs, nps)
```

```python
import functools

import jax
import jax.numpy as jnp
from jax import lax
from jax.experimental import pallas as pl
from jax.experimental.pallas import tpu as pltpu

D = 1024
N_IN = 5
N_ADA = 6
CONV_WIDTH = 31
CONV_PAD = 32
POOL_PAD = 16
HIST_STRIDE = 2
CONV_ROWS = 64
POOL_WINDOWS = (2, 4, 8, 16)
POOL_GROUP_W = D // 4
N_GROUPS = 4
EPG = 8
N_EXPERTS = N_GROUPS * EPG
N_ROUTE = 40
D_EXPERT = D // 2
EPS = 1e-6
LANES = 128
SUBLANES = 8
ROW_VREGS = D // LANES
MOE_ROWS = 128
MOE_STRIDE = MOE_ROWS + 8
MOE_LEAD = 1024
VMEM_LIMIT = 56 * 1024 * 1024

_F32 = jnp.float32
_BF16 = jnp.bfloat16


def _dot(a, b):
    return jnp.dot(a, b, preferred_element_type=_F32)


def _sigmoid(x):
    return jax.nn.sigmoid(x)


def _sigmoid_eup(x):
    return 0.5 * jnp.tanh(0.5 * x) + 0.5


def _round_bf16(x):
    return x.astype(_BF16).astype(_F32)


def _ada_kernel(c_ref, w_ref, b_ref, o_ref):
    c = c_ref[...]
    s = (c * _sigmoid(c)).astype(_BF16)
    o_ref[...] = _dot(s, w_ref[...].astype(_BF16)) + b_ref[...]


def _ada(c_all, w_ada, b_ada):
    nb = c_all.shape[0]
    return pl.pallas_call(
        _ada_kernel,
        grid=(N_ADA,),
        in_specs=[
            pl.BlockSpec((nb, D), lambda j: (0, 0)),
            pl.BlockSpec((D, D), lambda j: (0, j)),
            pl.BlockSpec((1, D), lambda j: (0, j)),
        ],
        out_specs=pl.BlockSpec((nb, D), lambda j: (0, j)),
        out_shape=jax.ShapeDtypeStruct((nb, N_ADA * D), _F32),
        name="ada",
    )(c_all, w_ada, b_ada)


def _mix_kernel(x_ref, mod_ref, hc_ref, hp_ref, gmix_ref, win_ref, w3_ref, wdw_ref, bdw_ref, gln_ref,
                bln_ref, wpw_ref, wpool_ref, spool_ref, wout_ref, gffn_ref, wr_ref, br_ref,
                x1_ref, n2t_ref, route_ref, ncs_ref, nps_ref,
                nb_scr, vbuf, ubuf, t1, t2, z3, ab_scr, pb_scr, *, tl, rc, start_pos):
    l = pl.program_id(1)
    nchunk = tl // rc
    slab = 3 * D // nchunk

    def chunks(body):
        def step(i, carry):
            body(pl.multiple_of(i * rc, rc))
            return carry
        lax.fori_loop(0, nchunk, step, 0)

    def trow(j, n):
        return pl.ds(HIST_STRIDE * j, n, stride=HIST_STRIDE)

    def hist_put(buf, j, val):
        for lc in range(ROW_VREGS):
            buf[lc, trow(j, val.shape[0]), :] = val[:, lc * LANES:(lc + 1) * LANES]

    def hist_get(buf, j, n):
        return jnp.concatenate([buf[lc, trow(j, n), :] for lc in range(ROW_VREGS)], axis=1)

    @pl.when(l == 0)
    def _():
        hist_put(vbuf, 0, _round_bf16(hc_ref[...]))
        hist_put(ubuf, 0, hp_ref[...])

    sh1 = mod_ref[:, 0 * D:1 * D]
    sc1 = mod_ref[:, 1 * D:2 * D]
    gt1 = mod_ref[:, 2 * D:3 * D]
    sh2 = mod_ref[:, 3 * D:4 * D]
    sc2 = mod_ref[:, 4 * D:5 * D]

    def norm1(r):
        xc = x_ref[pl.ds(r, rc), :]
        ms = jnp.mean(xc * xc, axis=-1, keepdims=True)
        n = xc * lax.rsqrt(ms + EPS) * gmix_ref[...]
        nb_scr[pl.ds(r, rc), :] = (n * (1.0 + sc1) + sh1).astype(_BF16)
    chunks(norm1)

    t1[...] = _dot(nb_scr[...], win_ref[:, 0 * D:1 * D])
    t2[...] = _dot(nb_scr[...], win_ref[:, 1 * D:2 * D])

    def glu(r):
        v = t1[pl.ds(r, rc), :] * _sigmoid_eup(t2[pl.ds(r, rc), :])
        hist_put(vbuf, CONV_PAD + r, _round_bf16(v))
    chunks(glu)
    ncs_ref[...] = t1[tl - CONV_PAD:tl, :] * _sigmoid_eup(t2[tl - CONV_PAD:tl, :])

    def z3_cols(rows, col0, width):
        si, off = divmod(col0, slab)
        assert off + width <= slab
        return z3[si, rows, off:off + width]

    def conv(c, carry):
        r = pl.multiple_of(c * rc, rc)
        for lc in range(ROW_VREGS):
            cs = slice(lc * LANES, (lc + 1) * LANES)
            taps = [jnp.broadcast_to(wdw_ref[k:k + 1, cs], (SUBLANES, LANES)) for k in range(CONV_WIDTH)]
            bias = jnp.broadcast_to(bdw_ref[:, cs], (SUBLANES, LANES))
            ngrp = min(rc, CONV_ROWS) // SUBLANES
            for r2 in range(0, rc, ngrp * SUBLANES):
                accs = [bias] * ngrp
                for o in range((ngrp - 1) * SUBLANES + CONV_WIDTH):
                    xw = vbuf[lc, trow(r + r2 + o + (CONV_PAD - CONV_WIDTH + 1), SUBLANES), :]
                    for i in range(ngrp):
                        k = o - i * SUBLANES
                        if 0 <= k < CONV_WIDTH:
                            accs[i] = accs[i] + taps[k] * xw
                for i in range(ngrp):
                    t1[pl.ds(r + r2 + i * SUBLANES, SUBLANES), cs] = accs[i]
        cv = t1[pl.ds(r, rc), :]
        mu = jnp.mean(cv, axis=-1, keepdims=True)
        xc = cv - mu
        a = xc * lax.rsqrt(jnp.mean(xc * xc, axis=-1, keepdims=True) + EPS) * gln_ref[...] + bln_ref[...]
        ab_scr[pl.ds(r, rc), :] = (a * _sigmoid(a)).astype(_BF16)
        z3[c] = _dot(nb_scr[...], w3_ref[c])
        return carry
    lax.fori_loop(0, nchunk, conv, 0)

    hist_put(ubuf, POOL_PAD, jnp.concatenate(
        [z3_cols(slice(None), q * LANES, LANES) for q in range(ROW_VREGS)], axis=1))
    t1[...] = _dot(ab_scr[...], wpw_ref[...])

    for c in range(nchunk):
        r = c * rc
        pos = start_pos + l * tl + r + lax.broadcasted_iota(jnp.int32, (rc, 1), 0)
        for lc in range(ROW_VREGS):
            cs = slice(lc * LANES, (lc + 1) * LANES)
            w = POOL_WINDOWS[lc * LANES // POOL_GROUP_W]
            u = z3_cols(slice(r, r + rc), lc * LANES, LANES)
            sw = u
            for i in range(1, w):
                sw = sw + ubuf[lc, trow(POOL_PAD + r - i, rc), :]
            cnt = jnp.minimum(pos + 1, w).astype(_F32)
            pb_scr[r:r + rc, cs] = (sw * (1.0 / cnt) - u).astype(_BF16)

    for g in range(N_GROUPS):
        gs = slice(g * POOL_GROUP_W, (g + 1) * POOL_GROUP_W)
        t2[:, gs] = _dot(pb_scr[:, gs], wpool_ref[g]) * spool_ref[:, gs]

    def mix(r):
        rs = pl.ds(r, rc)
        for q in range(D // POOL_GROUP_W):
            qs = slice(q * POOL_GROUP_W, (q + 1) * POOL_GROUP_W)
            ga = z3_cols(rs, D + q * POOL_GROUP_W, POOL_GROUP_W)
            gb = z3_cols(rs, 2 * D + q * POOL_GROUP_W, POOL_GROUP_W)
            ab_scr[rs, qs] = (_sigmoid_eup(ga) * t1[rs, qs] + _sigmoid_eup(gb) * t2[rs, qs]).astype(_BF16)
    chunks(mix)

    t1[...] = _dot(ab_scr[...], wout_ref[...])

    def resid(r):
        rs = pl.ds(r, rc)
        x1 = x_ref[rs, :] + gt1 * t1[rs, :]
        x1_ref[rs, :] = x1
        ms = jnp.mean(x1 * x1, axis=-1, keepdims=True)
        n2 = x1 * lax.rsqrt(ms + EPS) * gffn_ref[...] * (1.0 + sc2) + sh2
        t2[rs, :] = n2
        nb_scr[rs, :] = n2.astype(_BF16)
    chunks(resid)

    for j in range(ROW_VREGS):
        n2t_ref[pl.ds(j, tl, stride=ROW_VREGS), :] = t2[:, j * LANES:(j + 1) * LANES]

    lg = lax.dot_general(wr_ref[...], nb_scr[...], (((1,), (1,)), ((), ())),
                         preferred_element_type=_F32) + br_ref[...]
    lgg = lg[0:N_GROUPS, :]
    mg = jnp.max(lgg, axis=0, keepdims=True)
    p_g = 1.0 / jnp.sum(jnp.exp(lgg - mg), axis=0, keepdims=True)
    gi = lax.broadcasted_iota(jnp.int32, lgg.shape, 0)
    g_idx = jnp.min(jnp.where(lgg == mg, gi, N_GROUPS), axis=0, keepdims=True)
    sel = jnp.zeros((EPG, tl), _F32)
    for g in range(N_GROUPS):
        sel = jnp.where(g_idx == g, lg[N_GROUPS + g * EPG:N_GROUPS + (g + 1) * EPG, :], sel)
    ms = jnp.max(sel, axis=0, keepdims=True)
    es = jnp.exp(sel - ms)
    pe = es / jnp.sum(es, axis=0, keepdims=True)
    ei = lax.broadcasted_iota(jnp.int32, pe.shape, 0)
    v1 = jnp.max(pe, axis=0, keepdims=True)
    i1 = jnp.min(jnp.where(pe == v1, ei, EPG), axis=0, keepdims=True)
    pe2 = jnp.where(ei == i1, -1.0, pe)
    v2 = jnp.max(pe2, axis=0, keepdims=True)
    i2 = jnp.min(jnp.where(pe2 == v2, ei, EPG), axis=0, keepdims=True)
    scale = p_g / (v1 + v2)
    route_ref[0:1, :] = (g_idx * EPG + i1).astype(_F32)
    route_ref[1:2, :] = (g_idx * EPG + i2).astype(_F32)
    route_ref[2:3, :] = v1 * scale
    route_ref[3:4, :] = v2 * scale
    route_ref[4:8, :] = jnp.zeros((4, tl), _F32)

    utail = hist_get(ubuf, tl, POOL_PAD)
    nps_ref[...] = utail
    hist_put(vbuf, 0, hist_get(vbuf, tl, CONV_PAD))
    hist_put(ubuf, 0, utail)


def _const_spec(shape):
    nd = len(shape)
    return pl.BlockSpec(shape, lambda b, l: (0,) * nd, pipeline_mode=pl.Buffered(1))


def _mixer(x, mod3, boff, hc, hp, start_pos, wts, tl, rc):
    bsz, seq, _ = x.shape
    nl = seq // tl
    nchunk = tl // rc
    slab = 3 * D // nchunk
    w_in = wts[1]
    w3 = jnp.transpose(w_in[:, 2 * D:].reshape(D, nchunk, slab), (1, 0, 2))
    wts = (wts[0], w_in[:, :2 * D], w3) + tuple(wts[2:])
    kern = functools.partial(_mix_kernel, tl=tl, rc=rc, start_pos=start_pos)
    in_specs = [
        pl.BlockSpec((None, tl, D), lambda b, l: (b, l, 0)),
        pl.BlockSpec((None, 1, N_ADA * D), lambda b, l: (b + boff, 0, 0)),
        pl.BlockSpec((None, CONV_PAD, D), lambda b, l: (b, 0, 0)),
        pl.BlockSpec((None, POOL_PAD, D), lambda b, l: (b, 0, 0)),
    ] + [_const_spec(w.shape) for w in wts]
    out_specs = [
        pl.BlockSpec((None, tl, D), lambda b, l: (b, l, 0)),
        pl.BlockSpec((tl * ROW_VREGS, LANES), lambda b, l: (b * nl + l, 0)),
        pl.BlockSpec((None, SUBLANES, tl), lambda b, l: (b, 0, l)),
        pl.BlockSpec((None, CONV_PAD, D), lambda b, l: (b, 0, 0)),
        pl.BlockSpec((None, POOL_PAD, D), lambda b, l: (b, 0, 0)),
    ]
    out_shape = [
        jax.ShapeDtypeStruct((bsz, seq, D), _F32),
        jax.ShapeDtypeStruct((bsz * seq * ROW_VREGS, LANES), _F32),
        jax.ShapeDtypeStruct((bsz, SUBLANES, seq), _F32),
        jax.ShapeDtypeStruct((bsz, CONV_PAD, D), _F32),
        jax.ShapeDtypeStruct((bsz, POOL_PAD, D), _F32),
    ]
    scratch = [
        pltpu.VMEM((tl, D), _BF16),
        pltpu.VMEM((ROW_VREGS, HIST_STRIDE * (tl + CONV_PAD), LANES), _F32),
        pltpu.VMEM((ROW_VREGS, HIST_STRIDE * (tl + POOL_PAD), LANES), _F32),
        pltpu.VMEM((tl, D), _F32),
        pltpu.VMEM((tl, D), _F32),
        pltpu.VMEM((nchunk, tl, slab), _F32),
        pltpu.VMEM((tl, D), _BF16),
        pltpu.VMEM((tl, D), _BF16),
    ]
    return pl.pallas_call(
        kern,
        grid=(bsz, nl),
        in_specs=in_specs,
        out_specs=out_specs,
        out_shape=out_shape,
        scratch_shapes=scratch,
        compiler_params=pltpu.CompilerParams(
            dimension_semantics=("arbitrary", "arbitrary"), vmem_limit_bytes=VMEM_LIMIT),
        name="mixer",
    )(x, mod3, hc, hp, *wts)


def _moe_kernel(nblk_ref, bstart_ref, tok_ref, wt_ref, src_ref, wgu_ref, wdn_ref, out_ref,
                xc_scr, yc_scr, xb_scr, *, s_tok, unroll):
    s = pl.program_id(0)
    e = pl.program_id(1)

    def gather(blk):
        base = MOE_LEAD + blk * MOE_ROWS
        for mi in range(MOE_ROWS):
            t = jnp.minimum(tok_ref[base + mi], s_tok - 1)
            xc_scr[pl.ds(mi, ROW_VREGS, stride=MOE_STRIDE), :] = src_ref[t]

    def scatter(blk):
        base = MOE_LEAD + blk * MOE_ROWS
        for g0 in range(0, MOE_ROWS, unroll):
            ts = [tok_ref[base + g0 + i] for i in range(unroll)]
            vals = [out_ref[ts[i]] + wt_ref[base + g0 + i]
                    * yc_scr[pl.ds(g0 + i, ROW_VREGS, stride=MOE_STRIDE), :] for i in range(unroll)]
            for i in range(unroll):
                out_ref[ts[i]] = vals[i]

    @pl.when(e == 0)
    def _():
        out_ref[...] = jnp.zeros(out_ref.shape, _F32)
        yc_scr[...] = jnp.zeros(yc_scr.shape, _F32)
        gather(0)

    nb = nblk_ref[s * N_EXPERTS + e]
    b0 = bstart_ref[s * N_EXPERTS + e]

    def block(b, carry):
        blk = b0 + b
        xb_scr[...] = jnp.concatenate(
            [xc_scr[pl.ds(j * MOE_STRIDE, MOE_ROWS), :] for j in range(ROW_VREGS)], axis=1).astype(_BF16)
        gather(blk + 1)
        scatter(blk - 1)
        y = None
        for hs in (slice(0, D_EXPERT // 2), slice(D_EXPERT // 2, D_EXPERT)):
            g = _dot(xb_scr[...], wgu_ref[:, hs])
            u = _dot(xb_scr[...], wgu_ref[:, D_EXPERT + hs.start:D_EXPERT + hs.stop])
            act = (g * _sigmoid(g) * u).astype(_BF16)
            yh = _dot(act, wdn_ref[hs, :])
            y = yh if y is None else y + yh
        for j in range(ROW_VREGS):
            yc_scr[pl.ds(j * MOE_STRIDE, MOE_ROWS), :] = y[:, j * LANES:(j + 1) * LANES]
        return carry

    lax.fori_loop(0, nb, block, 0)

    @pl.when(e == N_EXPERTS - 1)
    def _():
        scatter(b0 + nb - 1)


def _moe(nblk, bstart, tok, wt, n2t, w_gu, w_dn, s_tok, pad_tok, p_max):
    n_super = n2t.shape[0] // s_tok
    kern = functools.partial(_moe_kernel, s_tok=s_tok, unroll=8)
    grid_spec = pltpu.PrefetchScalarGridSpec(
        num_scalar_prefetch=2,
        grid=(n_super, N_EXPERTS),
        in_specs=[
            pl.BlockSpec((p_max,), lambda s, e, *_: (s,), memory_space=pltpu.SMEM),
            pl.BlockSpec((p_max,), lambda s, e, *_: (s,), memory_space=pltpu.SMEM),
            pl.BlockSpec((s_tok, ROW_VREGS, LANES), lambda s, e, *_: (s, 0, 0),
                         pipeline_mode=pl.Buffered(1)),
            pl.BlockSpec((None, D, D), lambda s, e, *_: (e, 0, 0)),
            pl.BlockSpec((None, D_EXPERT, D), lambda s, e, *_: (e, 0, 0)),
        ],
        out_specs=pl.BlockSpec((None, s_tok + pad_tok, ROW_VREGS, LANES), lambda s, e, *_: (s, 0, 0, 0),
                               pipeline_mode=pl.Buffered(1)),
        scratch_shapes=[
            pltpu.VMEM((ROW_VREGS * MOE_STRIDE, LANES), _F32),
            pltpu.VMEM((ROW_VREGS * MOE_STRIDE, LANES), _F32),
            pltpu.VMEM((MOE_ROWS, D), _BF16),
        ],
    )
    return pl.pallas_call(
        kern,
        grid_spec=grid_spec,
        out_shape=jax.ShapeDtypeStruct((n_super, s_tok + pad_tok, ROW_VREGS, LANES), _F32),
        compiler_params=pltpu.CompilerParams(
            dimension_semantics=("arbitrary", "arbitrary"), vmem_limit_bytes=VMEM_LIMIT),
        name="moe",
    )(nblk, bstart, tok, wt, n2t, w_gu, w_dn)


def _dispatch(route, s_tok):
    bsz, _, seq = route.shape
    n_tok = bsz * seq
    n_super = n_tok // s_tok
    r = jnp.transpose(route[:, 0:4, :], (1, 0, 2)).reshape(4, n_super, s_tok)
    e_real = jnp.concatenate([r[0], r[1]], axis=1).astype(jnp.int32)
    w_real = jnp.concatenate([r[2], r[3]], axis=1)
    t_real = jnp.broadcast_to(jnp.tile(jnp.arange(s_tok, dtype=jnp.int32), 2)[None, :], e_real.shape)
    experts = jnp.arange(N_EXPERTS, dtype=jnp.int32)
    counts = jnp.sum(e_real[:, :, None] == experts, axis=1).astype(jnp.int32)
    n_pad = (-counts) % MOE_ROWS
    cand = jnp.arange(MOE_ROWS - 1, dtype=jnp.int32)
    key_pad = jnp.where(cand[None, None, :] < n_pad[:, :, None], experts[None, :, None], N_EXPERTS)
    key_pad = key_pad.reshape(n_super, N_EXPERTS * (MOE_ROWS - 1)).astype(jnp.int32)
    n_sort = 2 * s_tok + N_EXPERTS * (MOE_ROWS - 1)
    p_alloc = (n_sort + MOE_ROWS + 1023) // 1024 * 1024
    tok_bits = s_tok.bit_length()
    keys = jnp.concatenate([(e_real << tok_bits) | t_real, (key_pad << tok_bits) | s_tok], axis=1)
    wts = jnp.concatenate([w_real, jnp.zeros((n_super, n_sort - 2 * s_tok), _F32)], axis=1)
    keys, wt = lax.sort((keys, wts), dimension=1, num_keys=1)
    tok = keys & ((1 << tok_bits) - 1)
    tok = jnp.pad(tok, ((0, 0), (MOE_LEAD, p_alloc - n_sort)), constant_values=s_tok)
    wt = jnp.pad(wt, ((0, 0), (MOE_LEAD, p_alloc - n_sort)))
    p_alloc += MOE_LEAD
    pcounts = counts + n_pad
    pstarts = jnp.cumsum(pcounts, axis=1) - pcounts
    nblk = (pcounts // MOE_ROWS).reshape(-1)
    bstart = (pstarts // MOE_ROWS).reshape(-1)
    return nblk, bstart, tok.reshape(-1), wt.reshape(-1), p_alloc


def _fin_kernel(x1_ref, f_ref, mod_ref, g_ref, o_ref, *, tlf):
    f = jnp.concatenate(
        [f_ref[pl.ds(j, tlf, stride=ROW_VREGS), :] for j in range(ROW_VREGS)], axis=1)
    x2 = x1_ref[...] + mod_ref[...] * f
    ms = jnp.mean(x2 * x2, axis=-1, keepdims=True)
    o_ref[...] = x2 * lax.rsqrt(ms + EPS) * g_ref[...]


def _final(x1, acc2d, mod3, boff, g_final, s_tok, tlf):
    bsz, seq, _ = x1.shape
    n_tok = bsz * seq
    n_super = n_tok // s_tok
    per = s_tok // tlf
    kern = functools.partial(_fin_kernel, tlf=tlf)
    y = pl.pallas_call(
        kern,
        grid=(n_super, per),
        in_specs=[
            pl.BlockSpec((tlf, D), lambda s, i: (s * per + i, 0)),
            pl.BlockSpec((tlf * ROW_VREGS, LANES), lambda s, i: (s * (per + 1) + i, 0)),
            pl.BlockSpec((None, 1, D), lambda s, i: ((s * s_tok + i * tlf) // seq + boff, 0, N_ADA - 1)),
            pl.BlockSpec((1, D), lambda s, i: (0, 0)),
        ],
        out_specs=pl.BlockSpec((tlf, D), lambda s, i: (s * per + i, 0)),
        out_shape=jax.ShapeDtypeStruct((n_tok, D), _F32),
        name="final",
    )(x1.reshape(n_tok, D), acc2d, mod3, g_final)
    return y.reshape(bsz, seq, D)


def _stream(x, mod3, boff, hc, hp, start_pos, mix_w, w_gu, w_dn, g_final, tl, rc, s_tok, tlf):
    bsz, seq, _ = x.shape
    x1, n2t, route, ncs, nps = _mixer(x, mod3, boff, hc, hp, start_pos, mix_w, tl, rc)
    nblk, bstart, tok, wt, p_alloc = _dispatch(route, s_tok)
    n2t3 = n2t.reshape(bsz * seq, ROW_VREGS, LANES)
    acc = _moe(nblk, bstart, tok, wt, n2t3, w_gu, w_dn, s_tok, tlf, p_alloc)
    acc2d = acc.reshape(-1, LANES)
    y = _final(x1, acc2d, mod3, boff, g_final, s_tok, tlf)
    return y, ncs[None, :, CONV_PAD - (CONV_WIDTH - 1):, :], nps[None, :, 1:, :]


def kernel(x_prompt, x_sample, state_conv, state_pool, c_prompt, c_sample, w_ada, b_ada, g_mix, w_in, w_dw, b_dw, g_ln, b_ln, w_pw, w_pool, s_pool, w_out, g_ffn, w_rg, b_rg, w_re, b_re, w_gate, w_up, w_down, g_final):
    bp, lp, _ = x_prompt.shape
    bs, ls, _ = x_sample.shape
    past_len = 1024

    c_all = jnp.concatenate([c_prompt, c_sample], axis=0)
    mod = _ada(c_all, w_ada[0], b_ada[0][None, :])
    mod3 = mod[:, None, :]

    w_r = jnp.concatenate(
        [w_rg[0], w_re[0], jnp.zeros((D, N_ROUTE - N_GROUPS - N_EXPERTS), _F32)], axis=1).T.astype(_BF16)
    b_r = jnp.concatenate(
        [b_rg[0], b_re[0], jnp.zeros((N_ROUTE - N_GROUPS - N_EXPERTS,), _F32)])[:, None]
    w_dw_p = jnp.concatenate([w_dw[0], jnp.zeros((1, D), _F32)], axis=0)
    mix_w = (g_mix[0][None, :], w_in[0].astype(_BF16), w_dw_p, b_dw[0][None, :], g_ln[0][None, :],
             b_ln[0][None, :], w_pw[0].astype(_BF16), w_pool[0].astype(_BF16), s_pool[0][None, :],
             w_out[0].astype(_BF16), g_ffn[0][None, :], w_r, b_r)
    w_gu = jnp.concatenate([w_gate[0], w_up[0]], axis=2).astype(_BF16)
    w_dn = w_down[0].astype(_BF16)
    g_fin = g_final[None, :]

    hc_p = jnp.zeros((bp, CONV_PAD, D), _F32)
    hp_p = jnp.zeros((bp, POOL_PAD, D), _F32)
    hc_s = jnp.pad(state_conv[0], ((0, 0), (CONV_PAD - (CONV_WIDTH - 1), 0), (0, 0)))
    hp_s = jnp.pad(state_pool[0], ((0, 0), (1, 0), (0, 0)))

    y_p, ncp, npp = _stream(x_prompt, mod3, 0, hc_p, hp_p, 0, mix_w, w_gu, w_dn, g_fin,
                            tl=512, rc=128, s_tok=4096, tlf=256)
    y_s, ncs, nps = _stream(x_sample, mod3, bp, hc_s, hp_s, past_len, mix_w, w_gu, w_dn, g_fin,
                            tl=ls, rc=ls, s_tok=bs * ls, tlf=ls)
    return (y_p, y_s, ncp, npp, ncs, nps)
```

```python
import functools

import jax
import jax.numpy as jnp
from jax import lax
from jax.experimental import pallas as pl
from jax.experimental.pallas import tpu as pltpu

D = 1024
N_IN = 5
N_ADA = 6
CONV_WIDTH = 31
CONV_PAD = 32
POOL_PAD = 16
HIST_STRIDE = 2
CONV_ROWS = 64
POOL_WINDOWS = (2, 4, 8, 16)
POOL_GROUP_W = D // 4
N_GROUPS = 4
EPG = 8
N_EXPERTS = N_GROUPS * EPG
N_ROUTE = 40
D_EXPERT = D // 2
EPS = 1e-6
LANES = 128
SUBLANES = 8
ROW_VREGS = D // LANES
MOE_ROWS = 128
MOE_STRIDE = MOE_ROWS + 8
MOE_LEAD = 1024
MOE_DUMP = 8
MOE_EXPERTS_PER_STEP = 2
VMEM_LIMIT = 56 * 1024 * 1024

_F32 = jnp.float32
_BF16 = jnp.bfloat16


def _dot(a, b):
    return jnp.dot(a, b, preferred_element_type=_F32)


def _sigmoid(x):
    return jax.nn.sigmoid(x)


def _sigmoid_eup(x):
    return 0.5 * jnp.tanh(0.5 * x) + 0.5


def _round_bf16(x):
    return x.astype(_BF16).astype(_F32)


def _ada_kernel(c_ref, w_ref, b_ref, o_ref):
    c = c_ref[...]
    s = (c * _sigmoid(c)).astype(_BF16)
    o_ref[...] = _dot(s, w_ref[...].astype(_BF16)) + b_ref[...]


def _ada(c_all, w_ada, b_ada):
    nb = c_all.shape[0]
    return pl.pallas_call(
        _ada_kernel,
        grid=(N_ADA,),
        in_specs=[
            pl.BlockSpec((nb, D), lambda j: (0, 0)),
            pl.BlockSpec((D, D), lambda j: (0, j)),
            pl.BlockSpec((1, D), lambda j: (0, j)),
        ],
        out_specs=pl.BlockSpec((nb, D), lambda j: (0, j)),
        out_shape=jax.ShapeDtypeStruct((nb, N_ADA * D), _F32),
        name="ada",
    )(c_all, w_ada, b_ada)


def _mix_kernel(x_ref, mod_ref, hc_ref, hp_ref, gmix_ref, win_ref, w3_ref, wdw_ref, bdw_ref, gln_ref,
                bln_ref, wpw_ref, wpool_ref, spool_ref, wout_ref, gffn_ref, wr_ref, br_ref,
                x1_ref, n2t_ref, route_ref, ncs_ref, nps_ref,
                nb_scr, vbuf, ubuf, t1, t2, z3, ab_scr, pb_scr, *, tl, rc, start_pos):
    l = pl.program_id(1)
    nchunk = tl // rc
    slab = 3 * D // nchunk

    def chunks(body):
        def step(i, carry):
            body(pl.multiple_of(i * rc, rc))
            return carry
        lax.fori_loop(0, nchunk, step, 0)

    def trow(j, n):
        return pl.ds(HIST_STRIDE * j, n, stride=HIST_STRIDE)

    def hist_put(buf, j, val):
        for lc in range(ROW_VREGS):
            buf[lc, trow(j, val.shape[0]), :] = val[:, lc * LANES:(lc + 1) * LANES]

    def hist_get(buf, j, n):
        return jnp.concatenate([buf[lc, trow(j, n), :] for lc in range(ROW_VREGS)], axis=1)

    @pl.when(l == 0)
    def _():
        hist_put(vbuf, 0, _round_bf16(hc_ref[...]))
        hist_put(ubuf, 0, hp_ref[...])

    sh1 = mod_ref[:, 0 * D:1 * D]
    sc1 = mod_ref[:, 1 * D:2 * D]
    gt1 = mod_ref[:, 2 * D:3 * D]
    sh2 = mod_ref[:, 3 * D:4 * D]
    sc2 = mod_ref[:, 4 * D:5 * D]

    def norm1(r):
        xc = x_ref[pl.ds(r, rc), :]
        ms = jnp.mean(xc * xc, axis=-1, keepdims=True)
        n = xc * lax.rsqrt(ms + EPS) * gmix_ref[...]
        nb_scr[pl.ds(r, rc), :] = (n * (1.0 + sc1) + sh1).astype(_BF16)
    chunks(norm1)

    t1[...] = _dot(nb_scr[...], win_ref[:, 0 * D:1 * D])
    t2[...] = _dot(nb_scr[...], win_ref[:, 1 * D:2 * D])

    def glu(r):
        v = t1[pl.ds(r, rc), :] * _sigmoid_eup(t2[pl.ds(r, rc), :])
        hist_put(vbuf, CONV_PAD + r, _round_bf16(v))
    chunks(glu)
    ncs_ref[...] = t1[tl - CONV_PAD:tl, :] * _sigmoid_eup(t2[tl - CONV_PAD:tl, :])

    def z3_cols(rows, col0, width):
        si, off = divmod(col0, slab)
        assert off + width <= slab
        return z3[si, rows, off:off + width]

    def conv(c, carry):
        r = pl.multiple_of(c * rc, rc)
        for lc in range(ROW_VREGS):
            cs = slice(lc * LANES, (lc + 1) * LANES)
            taps = [jnp.broadcast_to(wdw_ref[k:k + 1, cs], (SUBLANES, LANES)) for k in range(CONV_WIDTH)]
            bias = jnp.broadcast_to(bdw_ref[:, cs], (SUBLANES, LANES))
            ngrp = min(rc, CONV_ROWS) // SUBLANES
            for r2 in range(0, rc, ngrp * SUBLANES):
                accs = [bias] * ngrp
                for o in range((ngrp - 1) * SUBLANES + CONV_WIDTH):
                    xw = vbuf[lc, trow(r + r2 + o + (CONV_PAD - CONV_WIDTH + 1), SUBLANES), :]
                    for i in range(ngrp):
                        k = o - i * SUBLANES
                        if 0 <= k < CONV_WIDTH:
                            accs[i] = accs[i] + taps[k] * xw
                for i in range(ngrp):
                    t1[pl.ds(r + r2 + i * SUBLANES, SUBLANES), cs] = accs[i]
        cv = t1[pl.ds(r, rc), :]
        mu = jnp.mean(cv, axis=-1, keepdims=True)
        xc = cv - mu
        a = xc * lax.rsqrt(jnp.mean(xc * xc, axis=-1, keepdims=True) + EPS) * gln_ref[...] + bln_ref[...]
        ab_scr[pl.ds(r, rc), :] = (a * _sigmoid(a)).astype(_BF16)
        z3[c] = _dot(nb_scr[...], w3_ref[c])
        return carry
    lax.fori_loop(0, nchunk, conv, 0)

    hist_put(ubuf, POOL_PAD, jnp.concatenate(
        [z3_cols(slice(None), q * LANES, LANES) for q in range(ROW_VREGS)], axis=1))
    t1[...] = _dot(ab_scr[...], wpw_ref[...])

    for c in range(nchunk):
        r = c * rc
        pos = start_pos + l * tl + r + lax.broadcasted_iota(jnp.int32, (rc, 1), 0)
        for lc in range(ROW_VREGS):
            cs = slice(lc * LANES, (lc + 1) * LANES)
            w = POOL_WINDOWS[lc * LANES // POOL_GROUP_W]
            u = z3_cols(slice(r, r + rc), lc * LANES, LANES)
            sw = u
            for i in range(1, w):
                sw = sw + ubuf[lc, trow(POOL_PAD + r - i, rc), :]
            cnt = jnp.minimum(pos + 1, w).astype(_F32)
            pb_scr[r:r + rc, cs] = (sw * (1.0 / cnt) - u).astype(_BF16)

    for g in range(N_GROUPS):
        gs = slice(g * POOL_GROUP_W, (g + 1) * POOL_GROUP_W)
        t2[:, gs] = _dot(pb_scr[:, gs], wpool_ref[g]) * spool_ref[:, gs]

    def mix(r):
        rs = pl.ds(r, rc)
        for q in range(D // POOL_GROUP_W):
            qs = slice(q * POOL_GROUP_W, (q + 1) * POOL_GROUP_W)
            ga = z3_cols(rs, D + q * POOL_GROUP_W, POOL_GROUP_W)
            gb = z3_cols(rs, 2 * D + q * POOL_GROUP_W, POOL_GROUP_W)
            ab_scr[rs, qs] = (_sigmoid_eup(ga) * t1[rs, qs] + _sigmoid_eup(gb) * t2[rs, qs]).astype(_BF16)
    chunks(mix)

    t1[...] = _dot(ab_scr[...], wout_ref[...])

    def resid(r):
        rs = pl.ds(r, rc)
        x1 = x_ref[rs, :] + gt1 * t1[rs, :]
        x1_ref[rs, :] = x1
        ms = jnp.mean(x1 * x1, axis=-1, keepdims=True)
        n2 = x1 * lax.rsqrt(ms + EPS) * gffn_ref[...] * (1.0 + sc2) + sh2
        nb_scr[rs, :] = n2.astype(_BF16)
        for j in range(ROW_VREGS):
            n2t_ref[pl.ds(r * ROW_VREGS + j, rc, stride=ROW_VREGS), :] = n2[:, j * LANES:(j + 1) * LANES]
    chunks(resid)

    lg = lax.dot_general(wr_ref[...], nb_scr[...], (((1,), (1,)), ((), ())),
                         preferred_element_type=_F32) + br_ref[...]
    lgg = lg[0:N_GROUPS, :]
    mg = jnp.max(lgg, axis=0, keepdims=True)
    p_g = 1.0 / jnp.sum(jnp.exp(lgg - mg), axis=0, keepdims=True)
    gi = lax.broadcasted_iota(jnp.int32, lgg.shape, 0)
    g_idx = jnp.min(jnp.where(lgg == mg, gi, N_GROUPS), axis=0, keepdims=True)
    sel = jnp.zeros((EPG, tl), _F32)
    for g in range(N_GROUPS):
        sel = jnp.where(g_idx == g, lg[N_GROUPS + g * EPG:N_GROUPS + (g + 1) * EPG, :], sel)
    ms = jnp.max(sel, axis=0, keepdims=True)
    es = jnp.exp(sel - ms)
    pe = es / jnp.sum(es, axis=0, keepdims=True)
    ei = lax.broadcasted_iota(jnp.int32, pe.shape, 0)
    v1 = jnp.max(pe, axis=0, keepdims=True)
    i1 = jnp.min(jnp.where(pe == v1, ei, EPG), axis=0, keepdims=True)
    pe2 = jnp.where(ei == i1, -1.0, pe)
    v2 = jnp.max(pe2, axis=0, keepdims=True)
    i2 = jnp.min(jnp.where(pe2 == v2, ei, EPG), axis=0, keepdims=True)
    scale = p_g / (v1 + v2)
    route_ref[0:1, :] = (g_idx * EPG + i1).astype(_F32)
    route_ref[1:2, :] = (g_idx * EPG + i2).astype(_F32)
    route_ref[2:3, :] = v1 * scale
    route_ref[3:4, :] = v2 * scale
    route_ref[4:8, :] = jnp.zeros((4, tl), _F32)

    utail = hist_get(ubuf, tl, POOL_PAD)
    nps_ref[...] = utail
    hist_put(vbuf, 0, hist_get(vbuf, tl, CONV_PAD))
    hist_put(ubuf, 0, utail)


def _const_spec(shape):
    nd = len(shape)
    return pl.BlockSpec(shape, lambda b, l: (0,) * nd, pipeline_mode=pl.Buffered(1))


def _mixer(x, mod3, boff, hc, hp, start_pos, wts, tl, rc):
    bsz, seq, _ = x.shape
    nl = seq // tl
    nchunk = tl // rc
    slab = 3 * D // nchunk
    w_in = wts[1]
    w3 = jnp.transpose(w_in[:, 2 * D:].reshape(D, nchunk, slab), (1, 0, 2))
    wts = (wts[0], w_in[:, :2 * D], w3) + tuple(wts[2:])
    kern = functools.partial(_mix_kernel, tl=tl, rc=rc, start_pos=start_pos)
    in_specs = [
        pl.BlockSpec((None, tl, D), lambda b, l: (b, l, 0)),
        pl.BlockSpec((None, 1, N_ADA * D), lambda b, l: (b + boff, 0, 0)),
        pl.BlockSpec((None, CONV_PAD, D), lambda b, l: (b, 0, 0)),
        pl.BlockSpec((None, POOL_PAD, D), lambda b, l: (b, 0, 0)),
    ] + [_const_spec(w.shape) for w in wts]
    out_specs = [
        pl.BlockSpec((None, tl, D), lambda b, l: (b, l, 0)),
        pl.BlockSpec((tl * ROW_VREGS, LANES), lambda b, l: (b * nl + l, 0)),
        pl.BlockSpec((None, SUBLANES, tl), lambda b, l: (b, 0, l)),
        pl.BlockSpec((None, CONV_PAD, D), lambda b, l: (b, 0, 0)),
        pl.BlockSpec((None, POOL_PAD, D), lambda b, l: (b, 0, 0)),
    ]
    out_shape = [
        jax.ShapeDtypeStruct((bsz, seq, D), _F32),
        jax.ShapeDtypeStruct((bsz * seq * ROW_VREGS, LANES), _F32),
        jax.ShapeDtypeStruct((bsz, SUBLANES, seq), _F32),
        jax.ShapeDtypeStruct((bsz, CONV_PAD, D), _F32),
        jax.ShapeDtypeStruct((bsz, POOL_PAD, D), _F32),
    ]
    scratch = [
        pltpu.VMEM((tl, D), _BF16),
        pltpu.VMEM((ROW_VREGS, HIST_STRIDE * (tl + CONV_PAD), LANES), _F32),
        pltpu.VMEM((ROW_VREGS, HIST_STRIDE * (tl + POOL_PAD), LANES), _F32),
        pltpu.VMEM((tl, D), _F32),
        pltpu.VMEM((tl, D), _F32),
        pltpu.VMEM((nchunk, tl, slab), _F32),
        pltpu.VMEM((tl, D), _BF16),
        pltpu.VMEM((tl, D), _BF16),
    ]
    return pl.pallas_call(
        kern,
        grid=(bsz, nl),
        in_specs=in_specs,
        out_specs=out_specs,
        out_shape=out_shape,
        scratch_shapes=scratch,
        compiler_params=pltpu.CompilerParams(
            dimension_semantics=("arbitrary", "arbitrary"), vmem_limit_bytes=VMEM_LIMIT),
        name="mixer",
    )(x, mod3, hc, hp, *wts)


def _moe_kernel(nblk_ref, bstart_ref, tok_ref, wt_ref, src_ref, wgu_ref, wdn_ref, out_ref,
                xc_scr, yc_scr, xb_scr, *, s_tok, unroll):
    s = pl.program_id(0)
    e = pl.program_id(1)

    def gather(blk):
        base = MOE_LEAD + blk * MOE_ROWS
        for mi in range(MOE_ROWS):
            t = jnp.minimum(tok_ref[base + mi], s_tok - 1)
            xc_scr[pl.ds(mi, ROW_VREGS, stride=MOE_STRIDE), :] = src_ref[t]

    def scatter(blk):
        base = MOE_LEAD + blk * MOE_ROWS
        for g0 in range(0, MOE_ROWS, unroll):
            ts = [tok_ref[base + g0 + i] for i in range(unroll)]
            vals = [out_ref[ts[i]] + wt_ref[base + g0 + i]
                    * yc_scr[pl.ds(g0 + i, ROW_VREGS, stride=MOE_STRIDE), :] for i in range(unroll)]
            for i in range(unroll):
                out_ref[ts[i]] = vals[i]

    @pl.when(e == 0)
    def _():
        out_ref[...] = jnp.zeros(out_ref.shape, _F32)
        yc_scr[...] = jnp.zeros(yc_scr.shape, _F32)
        gather(0)

    def block(ee, blk):
        xb_scr[...] = jnp.concatenate(
            [xc_scr[pl.ds(j * MOE_STRIDE, MOE_ROWS), :] for j in range(ROW_VREGS)], axis=1).astype(_BF16)
        gather(blk + 1)
        scatter(blk - 1)
        y = None
        for hs in (slice(0, D_EXPERT // 2), slice(D_EXPERT // 2, D_EXPERT)):
            g = _dot(xb_scr[...], wgu_ref[ee, :, hs])
            u = _dot(xb_scr[...], wgu_ref[ee, :, D_EXPERT + hs.start:D_EXPERT + hs.stop])
            act = (g * _sigmoid(g) * u).astype(_BF16)
            yh = _dot(act, wdn_ref[ee, hs, :])
            y = yh if y is None else y + yh
        for j in range(ROW_VREGS):
            yc_scr[pl.ds(j * MOE_STRIDE, MOE_ROWS), :] = y[:, j * LANES:(j + 1) * LANES]

    for ee in range(MOE_EXPERTS_PER_STEP):
        idx = s * N_EXPERTS + e * MOE_EXPERTS_PER_STEP + ee
        nb = nblk_ref[idx]
        b0 = bstart_ref[idx]

        def body(b, carry, ee=ee, b0=b0):
            block(ee, b0 + b)
            return carry
        lax.fori_loop(0, nb, body, 0)

    @pl.when(e == N_EXPERTS // MOE_EXPERTS_PER_STEP - 1)
    def _():
        scatter(b0 + nb - 1)


def _moe(nblk, bstart, tok, wt, n2t, w_gu, w_dn, s_tok, pad_tok, p_max):
    n_super = n2t.shape[0] // s_tok
    kern = functools.partial(_moe_kernel, s_tok=s_tok, unroll=8)
    grid_spec = pltpu.PrefetchScalarGridSpec(
        num_scalar_prefetch=2,
        grid=(n_super, N_EXPERTS // MOE_EXPERTS_PER_STEP),
        in_specs=[
            pl.BlockSpec((p_max,), lambda s, e, *_: (s,), memory_space=pltpu.SMEM),
            pl.BlockSpec((p_max,), lambda s, e, *_: (s,), memory_space=pltpu.SMEM),
            pl.BlockSpec((s_tok, ROW_VREGS, LANES), lambda s, e, *_: (s, 0, 0),
                         pipeline_mode=pl.Buffered(1)),
            pl.BlockSpec((MOE_EXPERTS_PER_STEP, D, D), lambda s, e, *_: (e, 0, 0)),
            pl.BlockSpec((MOE_EXPERTS_PER_STEP, D_EXPERT, D), lambda s, e, *_: (e, 0, 0)),
        ],
        out_specs=pl.BlockSpec((None, s_tok + pad_tok, ROW_VREGS, LANES), lambda s, e, *_: (s, 0, 0, 0),
                               pipeline_mode=pl.Buffered(1)),
        scratch_shapes=[
            pltpu.VMEM((ROW_VREGS * MOE_STRIDE, LANES), _F32),
            pltpu.VMEM((ROW_VREGS * MOE_STRIDE, LANES), _F32),
            pltpu.VMEM((MOE_ROWS, D), _BF16),
        ],
    )
    return pl.pallas_call(
        kern,
        grid_spec=grid_spec,
        out_shape=jax.ShapeDtypeStruct((n_super, s_tok + pad_tok, ROW_VREGS, LANES), _F32),
        compiler_params=pltpu.CompilerParams(
            dimension_semantics=("arbitrary", "arbitrary"), vmem_limit_bytes=VMEM_LIMIT),
        name="moe",
    )(nblk, bstart, tok, wt, n2t, w_gu, w_dn)


def _dispatch(route, s_tok):
    bsz, _, seq = route.shape
    n_tok = bsz * seq
    n_super = n_tok // s_tok
    r = jnp.transpose(route[:, 0:4, :], (1, 0, 2)).reshape(4, n_super, s_tok)
    e_real = jnp.concatenate([r[0], r[1]], axis=1).astype(jnp.int32)
    w_real = jnp.concatenate([r[2], r[3]], axis=1)
    t_real = jnp.broadcast_to(jnp.tile(jnp.arange(s_tok, dtype=jnp.int32), 2)[None, :], e_real.shape)
    experts = jnp.arange(N_EXPERTS, dtype=jnp.int32)
    counts = jnp.sum(e_real[:, :, None] == experts, axis=1).astype(jnp.int32)
    n_pad = (-counts) % MOE_ROWS
    cand = jnp.arange(MOE_ROWS - 1, dtype=jnp.int32)
    key_pad = jnp.where(cand[None, None, :] < n_pad[:, :, None], experts[None, :, None], N_EXPERTS)
    key_pad = key_pad.reshape(n_super, N_EXPERTS * (MOE_ROWS - 1)).astype(jnp.int32)
    n_sort = 2 * s_tok + N_EXPERTS * (MOE_ROWS - 1)
    p_alloc = (n_sort + MOE_ROWS + 1023) // 1024 * 1024
    tok_bits = s_tok.bit_length()
    keys = jnp.concatenate([(e_real << tok_bits) | t_real, (key_pad << tok_bits) | s_tok], axis=1)
    wts = jnp.concatenate([w_real, jnp.zeros((n_super, n_sort - 2 * s_tok), _F32)], axis=1)
    keys, wt = lax.sort((keys, wts), dimension=1, num_keys=1)
    tok = keys & ((1 << tok_bits) - 1)
    tok = jnp.pad(tok, ((0, 0), (MOE_LEAD, p_alloc - n_sort)), constant_values=s_tok)
    wt = jnp.pad(wt, ((0, 0), (MOE_LEAD, p_alloc - n_sort)))
    p_alloc += MOE_LEAD
    pcounts = counts + n_pad
    pstarts = jnp.cumsum(pcounts, axis=1) - pcounts
    nblk = (pcounts // MOE_ROWS).reshape(-1)
    bstart = (pstarts // MOE_ROWS).reshape(-1)
    return nblk, bstart, tok.reshape(-1), wt.reshape(-1), p_alloc


def _fin_kernel(x1_ref, f_ref, mod_ref, g_ref, o_ref, *, tlf, rcf):
    def body(i, carry):
        r = pl.multiple_of(i * rcf, rcf)
        f = jnp.concatenate(
            [f_ref[pl.ds(r * ROW_VREGS + j, rcf, stride=ROW_VREGS), :] for j in range(ROW_VREGS)], axis=1)
        x2 = x1_ref[pl.ds(r, rcf), :] + mod_ref[...] * f
        ms = jnp.mean(x2 * x2, axis=-1, keepdims=True)
        o_ref[pl.ds(r, rcf), :] = x2 * lax.rsqrt(ms + EPS) * g_ref[...]
        return carry
    lax.fori_loop(0, tlf // rcf, body, 0)


def _final(x1, acc, mod3, boff, g_final, s_tok, tlf):
    bsz, seq, _ = x1.shape
    n_tok = bsz * seq
    n_super = n_tok // s_tok
    per = s_tok // tlf
    kern = functools.partial(_fin_kernel, tlf=tlf, rcf=min(tlf, 128))
    acc3 = acc.reshape(n_super, -1, LANES)
    y = pl.pallas_call(
        kern,
        grid=(n_super, per),
        in_specs=[
            pl.BlockSpec((tlf, D), lambda s, i: (s * per + i, 0)),
            pl.BlockSpec((None, tlf * ROW_VREGS, LANES), lambda s, i: (s, i, 0)),
            pl.BlockSpec((None, 1, D), lambda s, i: ((s * s_tok + i * tlf) // seq + boff, 0, N_ADA - 1)),
            pl.BlockSpec((1, D), lambda s, i: (0, 0)),
        ],
        out_specs=pl.BlockSpec((tlf, D), lambda s, i: (s * per + i, 0)),
        out_shape=jax.ShapeDtypeStruct((n_tok, D), _F32),
        compiler_params=pltpu.CompilerParams(vmem_limit_bytes=VMEM_LIMIT),
        name="final",
    )(x1.reshape(n_tok, D), acc3, mod3, g_final)
    return y.reshape(bsz, seq, D)


def _stream(x, mod3, boff, hc, hp, start_pos, mix_w, w_gu, w_dn, g_final, tl, rc, s_tok, tlf):
    bsz, seq, _ = x.shape
    x1, n2t, route, ncs, nps = _mixer(x, mod3, boff, hc, hp, start_pos, mix_w, tl, rc)
    nblk, bstart, tok, wt, p_alloc = _dispatch(route, s_tok)
    n2t3 = n2t.reshape(bsz * seq, ROW_VREGS, LANES)
    acc = _moe(nblk, bstart, tok, wt, n2t3, w_gu, w_dn, s_tok, MOE_DUMP, p_alloc)
    y = _final(x1, acc, mod3, boff, g_final, s_tok, tlf)
    return y, ncs[None, :, CONV_PAD - (CONV_WIDTH - 1):, :], nps[None, :, 1:, :]


def kernel(x_prompt, x_sample, state_conv, state_pool, c_prompt, c_sample, w_ada, b_ada, g_mix, w_in, w_dw, b_dw, g_ln, b_ln, w_pw, w_pool, s_pool, w_out, g_ffn, w_rg, b_rg, w_re, b_re, w_gate, w_up, w_down, g_final):
    bp, lp, _ = x_prompt.shape
    bs, ls, _ = x_sample.shape
    past_len = 1024

    c_all = jnp.concatenate([c_prompt, c_sample], axis=0)
    mod = _ada(c_all, w_ada[0], b_ada[0][None, :])
    mod3 = mod[:, None, :]

    w_r = jnp.concatenate(
        [w_rg[0], w_re[0], jnp.zeros((D, N_ROUTE - N_GROUPS - N_EXPERTS), _F32)], axis=1).T.astype(_BF16)
    b_r = jnp.concatenate(
        [b_rg[0], b_re[0], jnp.zeros((N_ROUTE - N_GROUPS - N_EXPERTS,), _F32)])[:, None]
    w_dw_p = jnp.concatenate([w_dw[0], jnp.zeros((1, D), _F32)], axis=0)
    mix_w = (g_mix[0][None, :], w_in[0].astype(_BF16), w_dw_p, b_dw[0][None, :], g_ln[0][None, :],
             b_ln[0][None, :], w_pw[0].astype(_BF16), w_pool[0].astype(_BF16), s_pool[0][None, :],
             w_out[0].astype(_BF16), g_ffn[0][None, :], w_r, b_r)
    w_gu = jnp.concatenate([w_gate[0], w_up[0]], axis=2).astype(_BF16)
    w_dn = w_down[0].astype(_BF16)
    g_fin = g_final[None, :]

    hc_p = jnp.zeros((bp, CONV_PAD, D), _F32)
    hp_p = jnp.zeros((bp, POOL_PAD, D), _F32)
    hc_s = jnp.pad(state_conv[0], ((0, 0), (CONV_PAD - (CONV_WIDTH - 1), 0), (0, 0)))
    hp_s = jnp.pad(state_pool[0], ((0, 0), (1, 0), (0, 0)))

    y_p, ncp, npp = _stream(x_prompt, mod3, 0, hc_p, hp_p, 0, mix_w, w_gu, w_dn, g_fin,
                            tl=512, rc=128, s_tok=4096, tlf=1024)
    y_s, ncs, nps = _stream(x_sample, mod3, bp, hc_s, hp_s, past_len, mix_w, w_gu, w_dn, g_fin,
                            tl=ls, rc=ls, s_tok=bs * ls, tlf=ls)
    return (y_p, y_s, ncp, npp, ncs, nps)
```

```python
import functools

import jax
import jax.numpy as jnp
from jax import lax
from jax.experimental import pallas as pl
from jax.experimental.pallas import tpu as pltpu

D = 1024
N_IN = 5
N_ADA = 6
CONV_WIDTH = 31
CONV_PAD = 32
POOL_PAD = 16
HIST_STRIDE = 2
CONV_ROWS = 64
POOL_WINDOWS = (2, 4, 8, 16)
POOL_GROUP_W = D // 4
N_GROUPS = 4
EPG = 8
N_EXPERTS = N_GROUPS * EPG
N_ROUTE = 40
D_EXPERT = D // 2
EPS = 1e-6
LANES = 128
SUBLANES = 8
ROW_VREGS = D // LANES
MOE_ROWS = 128
MOE_STRIDE = MOE_ROWS + 8
MOE_LEAD = 1024
MOE_DUMP = 8
MOE_EXPERTS_PER_STEP = 2
VMEM_LIMIT = 56 * 1024 * 1024

_F32 = jnp.float32
_BF16 = jnp.bfloat16


def _dot(a, b):
    return jnp.dot(a, b, preferred_element_type=_F32)


def _sigmoid(x):
    return jax.nn.sigmoid(x)


def _sigmoid_eup(x):
    return 0.5 * jnp.tanh(0.5 * x) + 0.5


def _round_bf16(x):
    return x.astype(_BF16).astype(_F32)


def _ada_kernel(c_ref, w_ref, b_ref, o_ref):
    c = c_ref[...]
    s = (c * _sigmoid(c)).astype(_BF16)
    o_ref[...] = _dot(s, w_ref[...].astype(_BF16)) + b_ref[...]


def _ada(c_all, w_ada, b_ada):
    nb = c_all.shape[0]
    return pl.pallas_call(
        _ada_kernel,
        grid=(N_ADA,),
        in_specs=[
            pl.BlockSpec((nb, D), lambda j: (0, 0)),
            pl.BlockSpec((D, D), lambda j: (0, j)),
            pl.BlockSpec((1, D), lambda j: (0, j)),
        ],
        out_specs=pl.BlockSpec((nb, D), lambda j: (0, j)),
        out_shape=jax.ShapeDtypeStruct((nb, N_ADA * D), _F32),
        name="ada",
    )(c_all, w_ada, b_ada)


def _mix_kernel(x_ref, mod_ref, hc_ref, hp_ref, gmix_ref, win_ref, w3_ref, wdw_ref, bdw_ref, gln_ref,
                bln_ref, wpw_ref, wpool_ref, spool_ref, wout_ref, gffn_ref, wr_ref, br_ref,
                x1_ref, n2t_ref, route_ref, ncs_ref, nps_ref,
                nb_scr, vbuf, ubuf, t1, t2, z3, ab_scr, pb_scr, *, tl, rc, start_pos):
    l = pl.program_id(1)
    nchunk = tl // rc
    slab = 3 * D // nchunk

    def chunks(body):
        def step(i, carry):
            body(pl.multiple_of(i * rc, rc))
            return carry
        lax.fori_loop(0, nchunk, step, 0)

    def trow(j, n):
        return pl.ds(HIST_STRIDE * j, n, stride=HIST_STRIDE)

    def hist_put(buf, j, val):
        for lc in range(ROW_VREGS):
            buf[lc, trow(j, val.shape[0]), :] = val[:, lc * LANES:(lc + 1) * LANES]

    def hist_get(buf, j, n):
        return jnp.concatenate([buf[lc, trow(j, n), :] for lc in range(ROW_VREGS)], axis=1)

    @pl.when(l == 0)
    def _():
        hist_put(vbuf, 0, _round_bf16(hc_ref[...]))
        hist_put(ubuf, 0, hp_ref[...])

    sh1 = mod_ref[:, 0 * D:1 * D]
    sc1 = mod_ref[:, 1 * D:2 * D]
    gt1 = mod_ref[:, 2 * D:3 * D]
    sh2 = mod_ref[:, 3 * D:4 * D]
    sc2 = mod_ref[:, 4 * D:5 * D]

    def norm1(r):
        xc = x_ref[pl.ds(r, rc), :]
        ms = jnp.mean(xc * xc, axis=-1, keepdims=True)
        n = xc * lax.rsqrt(ms + EPS) * gmix_ref[...]
        nb_scr[pl.ds(r, rc), :] = (n * (1.0 + sc1) + sh1).astype(_BF16)
    chunks(norm1)

    t1[...] = _dot(nb_scr[...], win_ref[:, 0 * D:1 * D])
    t2[...] = _dot(nb_scr[...], win_ref[:, 1 * D:2 * D])

    def glu(r):
        v = t1[pl.ds(r, rc), :] * _sigmoid_eup(t2[pl.ds(r, rc), :])
        hist_put(vbuf, CONV_PAD + r, _round_bf16(v))
    chunks(glu)
    ncs_ref[...] = t1[tl - CONV_PAD:tl, :] * _sigmoid_eup(t2[tl - CONV_PAD:tl, :])

    def z3_cols(rows, col0, width):
        si, off = divmod(col0, slab)
        assert off + width <= slab
        return z3[si, rows, off:off + width]

    def conv(c, carry):
        r = pl.multiple_of(c * rc, rc)
        for lc in range(ROW_VREGS):
            cs = slice(lc * LANES, (lc + 1) * LANES)
            taps = [jnp.broadcast_to(wdw_ref[k:k + 1, cs], (SUBLANES, LANES)) for k in range(CONV_WIDTH)]
            bias = jnp.broadcast_to(bdw_ref[:, cs], (SUBLANES, LANES))
            ngrp = min(rc, CONV_ROWS) // SUBLANES
            for r2 in range(0, rc, ngrp * SUBLANES):
                accs = [bias] * ngrp
                for o in range((ngrp - 1) * SUBLANES + CONV_WIDTH):
                    xw = vbuf[lc, trow(r + r2 + o + (CONV_PAD - CONV_WIDTH + 1), SUBLANES), :]
                    for i in range(ngrp):
                        k = o - i * SUBLANES
                        if 0 <= k < CONV_WIDTH:
                            accs[i] = accs[i] + taps[k] * xw
                for i in range(ngrp):
                    t1[pl.ds(r + r2 + i * SUBLANES, SUBLANES), cs] = accs[i]
        cv = t1[pl.ds(r, rc), :]
        mu = jnp.mean(cv, axis=-1, keepdims=True)
        xc = cv - mu
        a = xc * lax.rsqrt(jnp.mean(xc * xc, axis=-1, keepdims=True) + EPS) * gln_ref[...] + bln_ref[...]
        ab_scr[pl.ds(r, rc), :] = (a * _sigmoid(a)).astype(_BF16)
        z3[c] = _dot(nb_scr[...], w3_ref[c])
        return carry
    lax.fori_loop(0, nchunk, conv, 0)

    hist_put(ubuf, POOL_PAD, jnp.concatenate(
        [z3_cols(slice(None), q * LANES, LANES) for q in range(ROW_VREGS)], axis=1))
    t1[...] = _dot(ab_scr[...], wpw_ref[...])

    for c in range(nchunk):
        r = c * rc
        pos = start_pos + l * tl + r + lax.broadcasted_iota(jnp.int32, (rc, 1), 0)
        for lc in range(ROW_VREGS):
            cs = slice(lc * LANES, (lc + 1) * LANES)
            w = POOL_WINDOWS[lc * LANES // POOL_GROUP_W]
            u = z3_cols(slice(r, r + rc), lc * LANES, LANES)
            sw = u
            for i in range(1, w):
                sw = sw + ubuf[lc, trow(POOL_PAD + r - i, rc), :]
            cnt = jnp.minimum(pos + 1, w).astype(_F32)
            pb_scr[r:r + rc, cs] = (sw * (1.0 / cnt) - u).astype(_BF16)

    for g in range(N_GROUPS):
        gs = slice(g * POOL_GROUP_W, (g + 1) * POOL_GROUP_W)
        t2[:, gs] = _dot(pb_scr[:, gs], wpool_ref[g]) * spool_ref[:, gs]

    def mix(r):
        rs = pl.ds(r, rc)
        for q in range(D // POOL_GROUP_W):
            qs = slice(q * POOL_GROUP_W, (q + 1) * POOL_GROUP_W)
            ga = z3_cols(rs, D + q * POOL_GROUP_W, POOL_GROUP_W)
            gb = z3_cols(rs, 2 * D + q * POOL_GROUP_W, POOL_GROUP_W)
            ab_scr[rs, qs] = (_sigmoid_eup(ga) * t1[rs, qs] + _sigmoid_eup(gb) * t2[rs, qs]).astype(_BF16)
    chunks(mix)

    t1[...] = _dot(ab_scr[...], wout_ref[...])

    def resid(r):
        rs = pl.ds(r, rc)
        x1 = x_ref[rs, :] + gt1 * t1[rs, :]
        x1_ref[rs, :] = x1
        ms = jnp.mean(x1 * x1, axis=-1, keepdims=True)
        n2 = x1 * lax.rsqrt(ms + EPS) * gffn_ref[...] * (1.0 + sc2) + sh2
        nb_scr[rs, :] = n2.astype(_BF16)
        for j in range(ROW_VREGS):
            n2t_ref[pl.ds(r * ROW_VREGS + j, rc, stride=ROW_VREGS), :] = n2[:, j * LANES:(j + 1) * LANES]
    chunks(resid)

    lg = lax.dot_general(wr_ref[...], nb_scr[...], (((1,), (1,)), ((), ())),
                         preferred_element_type=_F32) + br_ref[...]
    lgg = lg[0:N_GROUPS, :]
    mg = jnp.max(lgg, axis=0, keepdims=True)
    p_g = 1.0 / jnp.sum(jnp.exp(lgg - mg), axis=0, keepdims=True)
    gi = lax.broadcasted_iota(jnp.int32, lgg.shape, 0)
    g_idx = jnp.min(jnp.where(lgg == mg, gi, N_GROUPS), axis=0, keepdims=True)
    sel = jnp.zeros((EPG, tl), _F32)
    for g in range(N_GROUPS):
        sel = jnp.where(g_idx == g, lg[N_GROUPS + g * EPG:N_GROUPS + (g + 1) * EPG, :], sel)
    ms = jnp.max(sel, axis=0, keepdims=True)
    es = jnp.exp(sel - ms)
    pe = es / jnp.sum(es, axis=0, keepdims=True)
    ei = lax.broadcasted_iota(jnp.int32, pe.shape, 0)
    v1 = jnp.max(pe, axis=0, keepdims=True)
    i1 = jnp.min(jnp.where(pe == v1, ei, EPG), axis=0, keepdims=True)
    pe2 = jnp.where(ei == i1, -1.0, pe)
    v2 = jnp.max(pe2, axis=0, keepdims=True)
    i2 = jnp.min(jnp.where(pe2 == v2, ei, EPG), axis=0, keepdims=True)
    scale = p_g / (v1 + v2)
    route_ref[0:1, :] = (g_idx * EPG + i1).astype(_F32)
    route_ref[1:2, :] = (g_idx * EPG + i2).astype(_F32)
    route_ref[2:3, :] = v1 * scale
    route_ref[3:4, :] = v2 * scale
    route_ref[4:8, :] = jnp.zeros((4, tl), _F32)

    utail = hist_get(ubuf, tl, POOL_PAD)
    nps_ref[...] = utail
    hist_put(vbuf, 0, hist_get(vbuf, tl, CONV_PAD))
    hist_put(ubuf, 0, utail)


def _const_spec(shape):
    nd = len(shape)
    return pl.BlockSpec(shape, lambda b, l: (0,) * nd, pipeline_mode=pl.Buffered(1))


def _mixer(x, mod3, boff, hc, hp, start_pos, wts, tl, rc):
    bsz, seq, _ = x.shape
    nl = seq // tl
    nchunk = tl // rc
    slab = 3 * D // nchunk
    w_in = wts[1]
    w3 = jnp.transpose(w_in[:, 2 * D:].reshape(D, nchunk, slab), (1, 0, 2))
    wts = (wts[0], w_in[:, :2 * D], w3) + tuple(wts[2:])
    kern = functools.partial(_mix_kernel, tl=tl, rc=rc, start_pos=start_pos)
    in_specs = [
        pl.BlockSpec((None, tl, D), lambda b, l: (b, l, 0)),
        pl.BlockSpec((None, 1, N_ADA * D), lambda b, l: (b + boff, 0, 0)),
        pl.BlockSpec((None, CONV_PAD, D), lambda b, l: (b, 0, 0)),
        pl.BlockSpec((None, POOL_PAD, D), lambda b, l: (b, 0, 0)),
    ] + [_const_spec(w.shape) for w in wts]
    out_specs = [
        pl.BlockSpec((None, tl, D), lambda b, l: (b, l, 0)),
        pl.BlockSpec((tl * ROW_VREGS, LANES), lambda b, l: (b * nl + l, 0)),
        pl.BlockSpec((None, SUBLANES, tl), lambda b, l: (b, 0, l)),
        pl.BlockSpec((None, CONV_PAD, D), lambda b, l: (b, 0, 0)),
        pl.BlockSpec((None, POOL_PAD, D), lambda b, l: (b, 0, 0)),
    ]
    out_shape = [
        jax.ShapeDtypeStruct((bsz, seq, D), _F32),
        jax.ShapeDtypeStruct((bsz * seq * ROW_VREGS, LANES), _F32),
        jax.ShapeDtypeStruct((bsz, SUBLANES, seq), _F32),
        jax.ShapeDtypeStruct((bsz, CONV_PAD, D), _F32),
        jax.ShapeDtypeStruct((bsz, POOL_PAD, D), _F32),
    ]
    scratch = [
        pltpu.VMEM((tl, D), _BF16),
        pltpu.VMEM((ROW_VREGS, HIST_STRIDE * (tl + CONV_PAD), LANES), _F32),
        pltpu.VMEM((ROW_VREGS, HIST_STRIDE * (tl + POOL_PAD), LANES), _F32),
        pltpu.VMEM((tl, D), _F32),
        pltpu.VMEM((tl, D), _F32),
        pltpu.VMEM((nchunk, tl, slab), _F32),
        pltpu.VMEM((tl, D), _BF16),
        pltpu.VMEM((tl, D), _BF16),
    ]
    return pl.pallas_call(
        kern,
        grid=(bsz, nl),
        in_specs=in_specs,
        out_specs=out_specs,
        out_shape=out_shape,
        scratch_shapes=scratch,
        compiler_params=pltpu.CompilerParams(
            dimension_semantics=("arbitrary", "arbitrary"), vmem_limit_bytes=VMEM_LIMIT),
        name="mixer",
    )(x, mod3, hc, hp, *wts)


def _moe_kernel(nblk_ref, bstart_ref, tok_ref, wt_ref, src_ref, wgu_ref, wdn_ref, x1_ref, mod_ref, g_ref,
                y_ref, acc, xc_scr, yc_scr, xb_scr, *, s_tok, tlf, unroll):
    s = pl.program_id(0)
    e = pl.program_id(1)
    n_esteps = N_EXPERTS // MOE_EXPERTS_PER_STEP

    def arow(t):
        return pl.ds(pl.multiple_of(t * ROW_VREGS, ROW_VREGS), ROW_VREGS)

    def gather(blk):
        base = MOE_LEAD + blk * MOE_ROWS
        for mi in range(MOE_ROWS):
            t = jnp.minimum(tok_ref[base + mi], s_tok - 1)
            xc_scr[pl.ds(mi, ROW_VREGS, stride=MOE_STRIDE), :] = src_ref[t]

    def scatter(blk):
        base = MOE_LEAD + blk * MOE_ROWS
        for g0 in range(0, MOE_ROWS, unroll):
            ts = [tok_ref[base + g0 + i] for i in range(unroll)]
            vals = [acc[arow(ts[i]), :] + wt_ref[base + g0 + i]
                    * yc_scr[pl.ds(g0 + i, ROW_VREGS, stride=MOE_STRIDE), :] for i in range(unroll)]
            for i in range(unroll):
                acc[arow(ts[i]), :] = vals[i]

    @pl.when(e == 0)
    def _():
        acc[...] = jnp.zeros(acc.shape, _F32)
        yc_scr[...] = jnp.zeros(yc_scr.shape, _F32)
        gather(0)

    def block(ee, blk):
        xb_scr[...] = jnp.concatenate(
            [xc_scr[pl.ds(j * MOE_STRIDE, MOE_ROWS), :] for j in range(ROW_VREGS)], axis=1).astype(_BF16)
        gather(blk + 1)
        scatter(blk - 1)
        y = None
        for hs in (slice(0, D_EXPERT // 2), slice(D_EXPERT // 2, D_EXPERT)):
            g = _dot(xb_scr[...], wgu_ref[ee, :, hs])
            u = _dot(xb_scr[...], wgu_ref[ee, :, D_EXPERT + hs.start:D_EXPERT + hs.stop])
            act = (g * _sigmoid(g) * u).astype(_BF16)
            yh = _dot(act, wdn_ref[ee, hs, :])
            y = yh if y is None else y + yh
        for j in range(ROW_VREGS):
            yc_scr[pl.ds(j * MOE_STRIDE, MOE_ROWS), :] = y[:, j * LANES:(j + 1) * LANES]

    for ee in range(MOE_EXPERTS_PER_STEP):
        idx = s * N_EXPERTS + jnp.minimum(e, n_esteps - 1) * MOE_EXPERTS_PER_STEP + ee
        nb = jnp.where(e < n_esteps, nblk_ref[idx], 0)
        b0 = bstart_ref[idx]

        def body(b, carry, ee=ee, b0=b0):
            block(ee, b0 + b)
            return carry
        lax.fori_loop(0, nb, body, 0)

    @pl.when(e == n_esteps - 1)
    def _():
        scatter(b0 + nb - 1)

    @pl.when(e >= n_esteps)
    def _():
        row0 = (e - n_esteps) * tlf
        rcf = min(tlf, 128)

        def body(i, carry):
            r = pl.multiple_of(i * rcf, rcf)
            f = jnp.concatenate(
                [acc[pl.ds((row0 + r) * ROW_VREGS + j, rcf, stride=ROW_VREGS), :] for j in range(ROW_VREGS)],
                axis=1)
            x2 = x1_ref[pl.ds(r, rcf), :] + mod_ref[...] * f
            ms = jnp.mean(x2 * x2, axis=-1, keepdims=True)
            y_ref[pl.ds(r, rcf), :] = x2 * lax.rsqrt(ms + EPS) * g_ref[...]
            return carry
        lax.fori_loop(0, tlf // rcf, body, 0)


def _moe(nblk, bstart, tok, wt, n2t, w_gu, w_dn, x1, mod3, boff, g_final, s_tok, tlf, p_max):
    bsz, seq, _ = x1.shape
    n_tok = bsz * seq
    n_super = n_tok // s_tok
    n_esteps = N_EXPERTS // MOE_EXPERTS_PER_STEP
    per = s_tok // tlf
    kern = functools.partial(_moe_kernel, s_tok=s_tok, tlf=tlf, unroll=8)

    def estep(e):
        return jnp.minimum(e, n_esteps - 1)

    def ostep(s, e):
        return s * per + jnp.maximum(e - n_esteps, 0)

    grid_spec = pltpu.PrefetchScalarGridSpec(
        num_scalar_prefetch=2,
        grid=(n_super, n_esteps + per),
        in_specs=[
            pl.BlockSpec((p_max,), lambda s, e, *_: (s,), memory_space=pltpu.SMEM),
            pl.BlockSpec((p_max,), lambda s, e, *_: (s,), memory_space=pltpu.SMEM),
            pl.BlockSpec((s_tok, ROW_VREGS, LANES), lambda s, e, *_: (s, 0, 0),
                         pipeline_mode=pl.Buffered(1)),
            pl.BlockSpec((MOE_EXPERTS_PER_STEP, D, D), lambda s, e, *_: (estep(e), 0, 0)),
            pl.BlockSpec((MOE_EXPERTS_PER_STEP, D_EXPERT, D), lambda s, e, *_: (estep(e), 0, 0)),
            pl.BlockSpec((tlf, D), lambda s, e, *_: (ostep(s, e), 0)),
            pl.BlockSpec((None, 1, D), lambda s, e, *_: (ostep(s, e) * tlf // seq + boff, 0, N_ADA - 1)),
            pl.BlockSpec((1, D), lambda s, e, *_: (0, 0)),
        ],
        out_specs=pl.BlockSpec((tlf, D), lambda s, e, *_: (ostep(s, e), 0)),
        scratch_shapes=[
            pltpu.VMEM(((s_tok + MOE_DUMP) * ROW_VREGS, LANES), _F32),
            pltpu.VMEM((ROW_VREGS * MOE_STRIDE, LANES), _F32),
            pltpu.VMEM((ROW_VREGS * MOE_STRIDE, LANES), _F32),
            pltpu.VMEM((MOE_ROWS, D), _BF16),
        ],
    )
    y = pl.pallas_call(
        kern,
        grid_spec=grid_spec,
        out_shape=jax.ShapeDtypeStruct((n_tok, D), _F32),
        compiler_params=pltpu.CompilerParams(
            dimension_semantics=("arbitrary", "arbitrary"), vmem_limit_bytes=VMEM_LIMIT),
        name="moe",
    )(nblk, bstart, tok, wt, n2t, w_gu, w_dn, x1.reshape(n_tok, D), mod3, g_final)
    return y.reshape(bsz, seq, D)


def _dispatch(route, s_tok):
    bsz, _, seq = route.shape
    n_tok = bsz * seq
    n_super = n_tok // s_tok
    r = jnp.transpose(route[:, 0:4, :], (1, 0, 2)).reshape(4, n_super, s_tok)
    e_real = jnp.concatenate([r[0], r[1]], axis=1).astype(jnp.int32)
    w_real = jnp.concatenate([r[2], r[3]], axis=1)
    t_real = jnp.broadcast_to(jnp.tile(jnp.arange(s_tok, dtype=jnp.int32), 2)[None, :], e_real.shape)
    experts = jnp.arange(N_EXPERTS, dtype=jnp.int32)
    counts = jnp.sum(e_real[:, :, None] == experts, axis=1).astype(jnp.int32)
    n_pad = (-counts) % MOE_ROWS
    cand = jnp.arange(MOE_ROWS - 1, dtype=jnp.int32)
    key_pad = jnp.where(cand[None, None, :] < n_pad[:, :, None], experts[None, :, None], N_EXPERTS)
    key_pad = key_pad.reshape(n_super, N_EXPERTS * (MOE_ROWS - 1)).astype(jnp.int32)
    n_sort = 2 * s_tok + N_EXPERTS * (MOE_ROWS - 1)
    p_alloc = (n_sort + MOE_ROWS + 1023) // 1024 * 1024
    tok_bits = s_tok.bit_length()
    keys = jnp.concatenate([(e_real << tok_bits) | t_real, (key_pad << tok_bits) | s_tok], axis=1)
    wts = jnp.concatenate([w_real, jnp.zeros((n_super, n_sort - 2 * s_tok), _F32)], axis=1)
    keys, wt = lax.sort((keys, wts), dimension=1, num_keys=1)
    tok = keys & ((1 << tok_bits) - 1)
    tok = jnp.pad(tok, ((0, 0), (MOE_LEAD, p_alloc - n_sort)), constant_values=s_tok)
    wt = jnp.pad(wt, ((0, 0), (MOE_LEAD, p_alloc - n_sort)))
    p_alloc += MOE_LEAD
    pcounts = counts + n_pad
    pstarts = jnp.cumsum(pcounts, axis=1) - pcounts
    nblk = (pcounts // MOE_ROWS).reshape(-1)
    bstart = (pstarts // MOE_ROWS).reshape(-1)
    return nblk, bstart, tok.reshape(-1), wt.reshape(-1), p_alloc


def _stream(x, mod3, boff, hc, hp, start_pos, mix_w, w_gu, w_dn, g_final, tl, rc, s_tok, tlf):
    bsz, seq, _ = x.shape
    x1, n2t, route, ncs, nps = _mixer(x, mod3, boff, hc, hp, start_pos, mix_w, tl, rc)
    nblk, bstart, tok, wt, p_alloc = _dispatch(route, s_tok)
    n2t3 = n2t.reshape(bsz * seq, ROW_VREGS, LANES)
    y = _moe(nblk, bstart, tok, wt, n2t3, w_gu, w_dn, x1, mod3, boff, g_final, s_tok, tlf, p_alloc)
    return y, ncs[None, :, CONV_PAD - (CONV_WIDTH - 1):, :], nps[None, :, 1:, :]


def kernel(x_prompt, x_sample, state_conv, state_pool, c_prompt, c_sample, w_ada, b_ada, g_mix, w_in, w_dw, b_dw, g_ln, b_ln, w_pw, w_pool, s_pool, w_out, g_ffn, w_rg, b_rg, w_re, b_re, w_gate, w_up, w_down, g_final):
    bp, lp, _ = x_prompt.shape
    bs, ls, _ = x_sample.shape
    past_len = 1024

    c_all = jnp.concatenate([c_prompt, c_sample], axis=0)
    mod = _ada(c_all, w_ada[0], b_ada[0][None, :])
    mod3 = mod[:, None, :]

    w_r = jnp.concatenate(
        [w_rg[0], w_re[0], jnp.zeros((D, N_ROUTE - N_GROUPS - N_EXPERTS), _F32)], axis=1).T.astype(_BF16)
    b_r = jnp.concatenate(
        [b_rg[0], b_re[0], jnp.zeros((N_ROUTE - N_GROUPS - N_EXPERTS,), _F32)])[:, None]
    w_dw_p = jnp.concatenate([w_dw[0], jnp.zeros((1, D), _F32)], axis=0)
    mix_w = (g_mix[0][None, :], w_in[0].astype(_BF16), w_dw_p, b_dw[0][None, :], g_ln[0][None, :],
             b_ln[0][None, :], w_pw[0].astype(_BF16), w_pool[0].astype(_BF16), s_pool[0][None, :],
             w_out[0].astype(_BF16), g_ffn[0][None, :], w_r, b_r)
    w_gu = jnp.concatenate([w_gate[0], w_up[0]], axis=2).astype(_BF16)
    w_dn = w_down[0].astype(_BF16)
    g_fin = g_final[None, :]

    hc_p = jnp.zeros((bp, CONV_PAD, D), _F32)
    hp_p = jnp.zeros((bp, POOL_PAD, D), _F32)
    hc_s = jnp.pad(state_conv[0], ((0, 0), (CONV_PAD - (CONV_WIDTH - 1), 0), (0, 0)))
    hp_s = jnp.pad(state_pool[0], ((0, 0), (1, 0), (0, 0)))

    y_p, ncp, npp = _stream(x_prompt, mod3, 0, hc_p, hp_p, 0, mix_w, w_gu, w_dn, g_fin,
                            tl=512, rc=128, s_tok=4096, tlf=512)
    y_s, ncs, nps = _stream(x_sample, mod3, bp, hc_s, hp_s, past_len, mix_w, w_gu, w_dn, g_fin,
                            tl=ls, rc=ls, s_tok=bs * ls, tlf=ls)
    return (y_p, y_s, ncp, npp, ncs, nps)
```

```python
import functools

import jax
import jax.numpy as jnp
from jax import lax
from jax.experimental import pallas as pl
from jax.experimental.pallas import tpu as pltpu

D = 1024
N_IN = 5
N_ADA = 6
CONV_WIDTH = 31
CONV_PAD = 32
POOL_PAD = 16
HIST_STRIDE = 2
CONV_ROWS = 64
POOL_WINDOWS = (2, 4, 8, 16)
POOL_GROUP_W = D // 4
N_GROUPS = 4
EPG = 8
N_EXPERTS = N_GROUPS * EPG
N_ROUTE = 40
D_EXPERT = D // 2
EPS = 1e-6
LANES = 128
SUBLANES = 8
ROW_VREGS = D // LANES
MOE_ROWS = 144
MOE_STRIDE = MOE_ROWS + 8
MOE_LEAD = 1024
MOE_DUMP = 8
MOE_EXPERTS_PER_STEP = 2
VMEM_LIMIT = 56 * 1024 * 1024

_F32 = jnp.float32
_BF16 = jnp.bfloat16


def _dot(a, b):
    return jnp.dot(a, b, preferred_element_type=_F32)


def _sigmoid(x):
    return jax.nn.sigmoid(x)


def _sigmoid_eup(x):
    return 0.5 * jnp.tanh(0.5 * x) + 0.5


def _round_bf16(x):
    return x.astype(_BF16).astype(_F32)


def _ada_kernel(c_ref, w_ref, b_ref, o_ref):
    c = c_ref[...]
    s = (c * _sigmoid(c)).astype(_BF16)
    o_ref[...] = _dot(s, w_ref[...].astype(_BF16)) + b_ref[...]


def _ada(c_all, w_ada, b_ada):
    nb = c_all.shape[0]
    return pl.pallas_call(
        _ada_kernel,
        grid=(N_ADA,),
        in_specs=[
            pl.BlockSpec((nb, D), lambda j: (0, 0)),
            pl.BlockSpec((D, D), lambda j: (0, j)),
            pl.BlockSpec((1, D), lambda j: (0, j)),
        ],
        out_specs=pl.BlockSpec((nb, D), lambda j: (0, j)),
        out_shape=jax.ShapeDtypeStruct((nb, N_ADA * D), _F32),
        name="ada",
    )(c_all, w_ada, b_ada)


def _mix_kernel(x_ref, mod_ref, hc_ref, hp_ref, gmix_ref, win_ref, w3_ref, wdw_ref, bdw_ref, gln_ref,
                bln_ref, wpw_ref, wpool_ref, spool_ref, wout_ref, gffn_ref, wr_ref, br_ref,
                x1_ref, n2t_ref, route_ref, ncs_ref, nps_ref,
                nb_scr, vbuf, ubuf, t1, t2, z3, ab_scr, pb_scr, *, tl, rc, start_pos):
    l = pl.program_id(1)
    nchunk = tl // rc
    slab = 3 * D // nchunk

    def chunks(body):
        def step(i, carry):
            body(pl.multiple_of(i * rc, rc))
            return carry
        lax.fori_loop(0, nchunk, step, 0)

    def trow(j, n):
        return pl.ds(HIST_STRIDE * j, n, stride=HIST_STRIDE)

    def hist_put(buf, j, val):
        for lc in range(ROW_VREGS):
            buf[lc, trow(j, val.shape[0]), :] = val[:, lc * LANES:(lc + 1) * LANES]

    def hist_get(buf, j, n):
        return jnp.concatenate([buf[lc, trow(j, n), :] for lc in range(ROW_VREGS)], axis=1)

    @pl.when(l == 0)
    def _():
        hist_put(vbuf, 0, _round_bf16(hc_ref[...]))
        hist_put(ubuf, 0, hp_ref[...])

    sh1 = mod_ref[:, 0 * D:1 * D]
    sc1 = mod_ref[:, 1 * D:2 * D]
    gt1 = mod_ref[:, 2 * D:3 * D]
    sh2 = mod_ref[:, 3 * D:4 * D]
    sc2 = mod_ref[:, 4 * D:5 * D]

    def norm1(r):
        xc = x_ref[pl.ds(r, rc), :]
        ms = jnp.mean(xc * xc, axis=-1, keepdims=True)
        n = xc * lax.rsqrt(ms + EPS) * gmix_ref[...]
        nb_scr[pl.ds(r, rc), :] = (n * (1.0 + sc1) + sh1).astype(_BF16)
    chunks(norm1)

    t1[...] = _dot(nb_scr[...], win_ref[:, 0 * D:1 * D])
    t2[...] = _dot(nb_scr[...], win_ref[:, 1 * D:2 * D])

    def glu(r):
        v = t1[pl.ds(r, rc), :] * _sigmoid_eup(t2[pl.ds(r, rc), :])
        hist_put(vbuf, CONV_PAD + r, _round_bf16(v))
    chunks(glu)
    ncs_ref[...] = t1[tl - CONV_PAD:tl, :] * _sigmoid_eup(t2[tl - CONV_PAD:tl, :])

    def z3_cols(rows, col0, width):
        si, off = divmod(col0, slab)
        assert off + width <= slab
        return z3[si, rows, off:off + width]

    def conv(c, carry):
        r = pl.multiple_of(c * rc, rc)
        for lc in range(ROW_VREGS):
            cs = slice(lc * LANES, (lc + 1) * LANES)
            taps = [jnp.broadcast_to(wdw_ref[k:k + 1, cs], (SUBLANES, LANES)) for k in range(CONV_WIDTH)]
            bias = jnp.broadcast_to(bdw_ref[:, cs], (SUBLANES, LANES))
            ngrp = min(rc, CONV_ROWS) // SUBLANES
            for r2 in range(0, rc, ngrp * SUBLANES):
                accs = [bias] * ngrp
                for o in range((ngrp - 1) * SUBLANES + CONV_WIDTH):
                    xw = vbuf[lc, trow(r + r2 + o + (CONV_PAD - CONV_WIDTH + 1), SUBLANES), :]
                    for i in range(ngrp):
                        k = o - i * SUBLANES
                        if 0 <= k < CONV_WIDTH:
                            accs[i] = accs[i] + taps[k] * xw
                for i in range(ngrp):
                    t1[pl.ds(r + r2 + i * SUBLANES, SUBLANES), cs] = accs[i]
        cv = t1[pl.ds(r, rc), :]
        mu = jnp.mean(cv, axis=-1, keepdims=True)
        xc = cv - mu
        a = xc * lax.rsqrt(jnp.mean(xc * xc, axis=-1, keepdims=True) + EPS) * gln_ref[...] + bln_ref[...]
        ab_scr[pl.ds(r, rc), :] = (a * _sigmoid(a)).astype(_BF16)
        z3[c] = _dot(nb_scr[...], w3_ref[c])
        return carry
    lax.fori_loop(0, nchunk, conv, 0)

    hist_put(ubuf, POOL_PAD, jnp.concatenate(
        [z3_cols(slice(None), q * LANES, LANES) for q in range(ROW_VREGS)], axis=1))
    t1[...] = _dot(ab_scr[...], wpw_ref[...])

    for c in range(nchunk):
        r = c * rc
        pos = start_pos + l * tl + r + lax.broadcasted_iota(jnp.int32, (rc, 1), 0)
        for lc in range(ROW_VREGS):
            cs = slice(lc * LANES, (lc + 1) * LANES)
            w = POOL_WINDOWS[lc * LANES // POOL_GROUP_W]
            u = z3_cols(slice(r, r + rc), lc * LANES, LANES)
            sw = u
            for i in range(1, w):
                sw = sw + ubuf[lc, trow(POOL_PAD + r - i, rc), :]
            cnt = jnp.minimum(pos + 1, w).astype(_F32)
            pb_scr[r:r + rc, cs] = (sw * (1.0 / cnt) - u).astype(_BF16)

    for g in range(N_GROUPS):
        gs = slice(g * POOL_GROUP_W, (g + 1) * POOL_GROUP_W)
        t2[:, gs] = _dot(pb_scr[:, gs], wpool_ref[g]) * spool_ref[:, gs]

    def mix(r):
        rs = pl.ds(r, rc)
        for q in range(D // POOL_GROUP_W):
            qs = slice(q * POOL_GROUP_W, (q + 1) * POOL_GROUP_W)
            ga = z3_cols(rs, D + q * POOL_GROUP_W, POOL_GROUP_W)
            gb = z3_cols(rs, 2 * D + q * POOL_GROUP_W, POOL_GROUP_W)
            ab_scr[rs, qs] = (_sigmoid_eup(ga) * t1[rs, qs] + _sigmoid_eup(gb) * t2[rs, qs]).astype(_BF16)
    chunks(mix)

    t1[...] = _dot(ab_scr[...], wout_ref[...])

    def resid(r):
        rs = pl.ds(r, rc)
        x1 = x_ref[rs, :] + gt1 * t1[rs, :]
        x1_ref[rs, :] = x1
        ms = jnp.mean(x1 * x1, axis=-1, keepdims=True)
        n2 = x1 * lax.rsqrt(ms + EPS) * gffn_ref[...] * (1.0 + sc2) + sh2
        nb_scr[rs, :] = n2.astype(_BF16)
        for j in range(ROW_VREGS):
            n2t_ref[pl.ds(r * ROW_VREGS + j, rc, stride=ROW_VREGS), :] = n2[:, j * LANES:(j + 1) * LANES]
    chunks(resid)

    lg = lax.dot_general(wr_ref[...], nb_scr[...], (((1,), (1,)), ((), ())),
                         preferred_element_type=_F32) + br_ref[...]
    lgg = lg[0:N_GROUPS, :]
    mg = jnp.max(lgg, axis=0, keepdims=True)
    p_g = 1.0 / jnp.sum(jnp.exp(lgg - mg), axis=0, keepdims=True)
    gi = lax.broadcasted_iota(jnp.int32, lgg.shape, 0)
    g_idx = jnp.min(jnp.where(lgg == mg, gi, N_GROUPS), axis=0, keepdims=True)
    sel = jnp.zeros((EPG, tl), _F32)
    for g in range(N_GROUPS):
        sel = jnp.where(g_idx == g, lg[N_GROUPS + g * EPG:N_GROUPS + (g + 1) * EPG, :], sel)
    ms = jnp.max(sel, axis=0, keepdims=True)
    es = jnp.exp(sel - ms)
    pe = es / jnp.sum(es, axis=0, keepdims=True)
    ei = lax.broadcasted_iota(jnp.int32, pe.shape, 0)
    v1 = jnp.max(pe, axis=0, keepdims=True)
    i1 = jnp.min(jnp.where(pe == v1, ei, EPG), axis=0, keepdims=True)
    pe2 = jnp.where(ei == i1, -1.0, pe)
    v2 = jnp.max(pe2, axis=0, keepdims=True)
    i2 = jnp.min(jnp.where(pe2 == v2, ei, EPG), axis=0, keepdims=True)
    scale = p_g / (v1 + v2)
    route_ref[0:1, :] = (g_idx * EPG + i1).astype(_F32)
    route_ref[1:2, :] = (g_idx * EPG + i2).astype(_F32)
    route_ref[2:3, :] = v1 * scale
    route_ref[3:4, :] = v2 * scale
    route_ref[4:8, :] = jnp.zeros((4, tl), _F32)

    utail = hist_get(ubuf, tl, POOL_PAD)
    nps_ref[...] = utail
    hist_put(vbuf, 0, hist_get(vbuf, tl, CONV_PAD))
    hist_put(ubuf, 0, utail)


def _const_spec(shape):
    nd = len(shape)
    return pl.BlockSpec(shape, lambda b, l: (0,) * nd, pipeline_mode=pl.Buffered(1))


def _mixer(x, mod3, boff, hc, hp, start_pos, wts, tl, rc):
    bsz, seq, _ = x.shape
    nl = seq // tl
    nchunk = tl // rc
    slab = 3 * D // nchunk
    w_in = wts[1]
    w3 = jnp.transpose(w_in[:, 2 * D:].reshape(D, nchunk, slab), (1, 0, 2))
    wts = (wts[0], w_in[:, :2 * D], w3) + tuple(wts[2:])
    kern = functools.partial(_mix_kernel, tl=tl, rc=rc, start_pos=start_pos)
    in_specs = [
        pl.BlockSpec((None, tl, D), lambda b, l: (b, l, 0)),
        pl.BlockSpec((None, 1, N_ADA * D), lambda b, l: (b + boff, 0, 0)),
        pl.BlockSpec((None, CONV_PAD, D), lambda b, l: (b, 0, 0)),
        pl.BlockSpec((None, POOL_PAD, D), lambda b, l: (b, 0, 0)),
    ] + [_const_spec(w.shape) for w in wts]
    out_specs = [
        pl.BlockSpec((None, tl, D), lambda b, l: (b, l, 0)),
        pl.BlockSpec((tl * ROW_VREGS, LANES), lambda b, l: (b * nl + l, 0)),
        pl.BlockSpec((None, SUBLANES, tl), lambda b, l: (b, 0, l)),
        pl.BlockSpec((None, CONV_PAD, D), lambda b, l: (b, 0, 0)),
        pl.BlockSpec((None, POOL_PAD, D), lambda b, l: (b, 0, 0)),
    ]
    out_shape = [
        jax.ShapeDtypeStruct((bsz, seq, D), _F32),
        jax.ShapeDtypeStruct((bsz * seq * ROW_VREGS, LANES), _F32),
        jax.ShapeDtypeStruct((bsz, SUBLANES, seq), _F32),
        jax.ShapeDtypeStruct((bsz, CONV_PAD, D), _F32),
        jax.ShapeDtypeStruct((bsz, POOL_PAD, D), _F32),
    ]
    scratch = [
        pltpu.VMEM((tl, D), _BF16),
        pltpu.VMEM((ROW_VREGS, HIST_STRIDE * (tl + CONV_PAD), LANES), _F32),
        pltpu.VMEM((ROW_VREGS, HIST_STRIDE * (tl + POOL_PAD), LANES), _F32),
        pltpu.VMEM((tl, D), _F32),
        pltpu.VMEM((tl, D), _F32),
        pltpu.VMEM((nchunk, tl, slab), _F32),
        pltpu.VMEM((tl, D), _BF16),
        pltpu.VMEM((tl, D), _BF16),
    ]
    return pl.pallas_call(
        kern,
        grid=(bsz, nl),
        in_specs=in_specs,
        out_specs=out_specs,
        out_shape=out_shape,
        scratch_shapes=scratch,
        compiler_params=pltpu.CompilerParams(
            dimension_semantics=("arbitrary", "arbitrary"), vmem_limit_bytes=VMEM_LIMIT),
        name="mixer",
    )(x, mod3, hc, hp, *wts)


def _moe_kernel(nblk_ref, bstart_ref, tok_ref, wt_ref, src_ref, wgu_ref, wdn_ref, x1_ref, mod_ref, g_ref,
                y_ref, acc, xc_scr, yc_scr, xb_scr, *, s_tok, tlf, unroll):
    s = pl.program_id(0)
    e = pl.program_id(1)
    n_esteps = N_EXPERTS // MOE_EXPERTS_PER_STEP

    def arow(t):
        return pl.ds(pl.multiple_of(t * ROW_VREGS, ROW_VREGS), ROW_VREGS)

    def gather(blk):
        base = MOE_LEAD + blk * MOE_ROWS
        for mi in range(MOE_ROWS):
            t = jnp.minimum(tok_ref[base + mi], s_tok - 1)
            xc_scr[pl.ds(mi, ROW_VREGS, stride=MOE_STRIDE), :] = src_ref[t]

    def scatter(blk):
        base = MOE_LEAD + blk * MOE_ROWS
        for g0 in range(0, MOE_ROWS, unroll):
            ts = [tok_ref[base + g0 + i] for i in range(unroll)]
            vals = [acc[arow(ts[i]), :] + wt_ref[base + g0 + i]
                    * yc_scr[pl.ds(g0 + i, ROW_VREGS, stride=MOE_STRIDE), :] for i in range(unroll)]
            for i in range(unroll):
                acc[arow(ts[i]), :] = vals[i]

    @pl.when(e == 0)
    def _():
        acc[...] = jnp.zeros(acc.shape, _F32)
        yc_scr[...] = jnp.zeros(yc_scr.shape, _F32)
        gather(0)

    def block(ee, blk):
        xb_scr[...] = jnp.concatenate(
            [xc_scr[pl.ds(j * MOE_STRIDE, MOE_ROWS), :] for j in range(ROW_VREGS)], axis=1).astype(_BF16)
        gather(blk + 1)
        scatter(blk - 1)
        y = None
        for hs in (slice(0, D_EXPERT // 2), slice(D_EXPERT // 2, D_EXPERT)):
            g = _dot(xb_scr[...], wgu_ref[ee, :, hs])
            u = _dot(xb_scr[...], wgu_ref[ee, :, D_EXPERT + hs.start:D_EXPERT + hs.stop])
            act = (g * _sigmoid(g) * u).astype(_BF16)
            yh = _dot(act, wdn_ref[ee, hs, :])
            y = yh if y is None else y + yh
        for j in range(ROW_VREGS):
            yc_scr[pl.ds(j * MOE_STRIDE, MOE_ROWS), :] = y[:, j * LANES:(j + 1) * LANES]

    for ee in range(MOE_EXPERTS_PER_STEP):
        idx = s * N_EXPERTS + jnp.minimum(e, n_esteps - 1) * MOE_EXPERTS_PER_STEP + ee
        nb = jnp.where(e < n_esteps, nblk_ref[idx], 0)
        b0 = bstart_ref[idx]

        def body(b, carry, ee=ee, b0=b0):
            block(ee, b0 + b)
            return carry
        lax.fori_loop(0, nb, body, 0)

    @pl.when(e == n_esteps - 1)
    def _():
        scatter(b0 + nb - 1)

    @pl.when(e >= n_esteps)
    def _():
        row0 = (e - n_esteps) * tlf
        rcf = min(tlf, 128)

        def body(i, carry):
            r = pl.multiple_of(i * rcf, rcf)
            f = jnp.concatenate(
                [acc[pl.ds((row0 + r) * ROW_VREGS + j, rcf, stride=ROW_VREGS), :] for j in range(ROW_VREGS)],
                axis=1)
            x2 = x1_ref[pl.ds(r, rcf), :] + mod_ref[...] * f
            ms = jnp.mean(x2 * x2, axis=-1, keepdims=True)
            y_ref[pl.ds(r, rcf), :] = x2 * lax.rsqrt(ms + EPS) * g_ref[...]
            return carry
        lax.fori_loop(0, tlf // rcf, body, 0)


def _moe(nblk, bstart, tok, wt, n2t, w_gu, w_dn, x1, mod3, boff, g_final, s_tok, tlf, p_max):
    bsz, seq, _ = x1.shape
    n_tok = bsz * seq
    n_super = n_tok // s_tok
    n_esteps = N_EXPERTS // MOE_EXPERTS_PER_STEP
    per = s_tok // tlf
    kern = functools.partial(_moe_kernel, s_tok=s_tok, tlf=tlf, unroll=8)

    def estep(e):
        return jnp.minimum(e, n_esteps - 1)

    def ostep(s, e):
        return s * per + jnp.maximum(e - n_esteps, 0)

    grid_spec = pltpu.PrefetchScalarGridSpec(
        num_scalar_prefetch=2,
        grid=(n_super, n_esteps + per),
        in_specs=[
            pl.BlockSpec((p_max,), lambda s, e, *_: (s,), memory_space=pltpu.SMEM),
            pl.BlockSpec((p_max,), lambda s, e, *_: (s,), memory_space=pltpu.SMEM),
            pl.BlockSpec((s_tok, ROW_VREGS, LANES), lambda s, e, *_: (s, 0, 0),
                         pipeline_mode=pl.Buffered(1)),
            pl.BlockSpec((MOE_EXPERTS_PER_STEP, D, D), lambda s, e, *_: (estep(e), 0, 0)),
            pl.BlockSpec((MOE_EXPERTS_PER_STEP, D_EXPERT, D), lambda s, e, *_: (estep(e), 0, 0)),
            pl.BlockSpec((tlf, D), lambda s, e, *_: (ostep(s, e), 0)),
            pl.BlockSpec((None, 1, D), lambda s, e, *_: (ostep(s, e) * tlf // seq + boff, 0, N_ADA - 1)),
            pl.BlockSpec((1, D), lambda s, e, *_: (0, 0)),
        ],
        out_specs=pl.BlockSpec((tlf, D), lambda s, e, *_: (ostep(s, e), 0)),
        scratch_shapes=[
            pltpu.VMEM(((s_tok + MOE_DUMP) * ROW_VREGS, LANES), _F32),
            pltpu.VMEM((ROW_VREGS * MOE_STRIDE, LANES), _F32),
            pltpu.VMEM((ROW_VREGS * MOE_STRIDE, LANES), _F32),
            pltpu.VMEM((MOE_ROWS, D), _BF16),
        ],
    )
    y = pl.pallas_call(
        kern,
        grid_spec=grid_spec,
        out_shape=jax.ShapeDtypeStruct((n_tok, D), _F32),
        compiler_params=pltpu.CompilerParams(
            dimension_semantics=("arbitrary", "arbitrary"), vmem_limit_bytes=VMEM_LIMIT),
        name="moe",
    )(nblk, bstart, tok, wt, n2t, w_gu, w_dn, x1.reshape(n_tok, D), mod3, g_final)
    return y.reshape(bsz, seq, D)


def _dispatch(route, s_tok):
    bsz, _, seq = route.shape
    n_tok = bsz * seq
    n_super = n_tok // s_tok
    r = jnp.transpose(route[:, 0:4, :], (1, 0, 2)).reshape(4, n_super, s_tok)
    e_real = jnp.concatenate([r[0], r[1]], axis=1).astype(jnp.int32)
    w_real = jnp.concatenate([r[2], r[3]], axis=1)
    t_real = jnp.broadcast_to(jnp.tile(jnp.arange(s_tok, dtype=jnp.int32), 2)[None, :], e_real.shape)
    experts = jnp.arange(N_EXPERTS, dtype=jnp.int32)
    counts = jnp.sum(e_real[:, :, None] == experts, axis=1).astype(jnp.int32)
    n_pad = (-counts) % MOE_ROWS
    cand = jnp.arange(MOE_ROWS - 1, dtype=jnp.int32)
    key_pad = jnp.where(cand[None, None, :] < n_pad[:, :, None], experts[None, :, None], N_EXPERTS)
    key_pad = key_pad.reshape(n_super, N_EXPERTS * (MOE_ROWS - 1)).astype(jnp.int32)
    n_sort = 2 * s_tok + N_EXPERTS * (MOE_ROWS - 1)
    p_alloc = (n_sort + MOE_ROWS + 1023) // 1024 * 1024
    tok_bits = s_tok.bit_length()
    keys = jnp.concatenate([(e_real << tok_bits) | t_real, (key_pad << tok_bits) | s_tok], axis=1)
    wts = jnp.concatenate([w_real, jnp.zeros((n_super, n_sort - 2 * s_tok), _F32)], axis=1)
    keys, wt = lax.sort((keys, wts), dimension=1, num_keys=1)
    tok = keys & ((1 << tok_bits) - 1)
    tok = jnp.pad(tok, ((0, 0), (MOE_LEAD, p_alloc - n_sort)), constant_values=s_tok)
    wt = jnp.pad(wt, ((0, 0), (MOE_LEAD, p_alloc - n_sort)))
    p_alloc += MOE_LEAD
    pcounts = counts + n_pad
    pstarts = jnp.cumsum(pcounts, axis=1) - pcounts
    nblk = (pcounts // MOE_ROWS).reshape(-1)
    bstart = (pstarts // MOE_ROWS).reshape(-1)
    return nblk, bstart, tok.reshape(-1), wt.reshape(-1), p_alloc


def _stream(x, mod3, boff, hc, hp, start_pos, mix_w, w_gu, w_dn, g_final, tl, rc, s_tok, tlf):
    bsz, seq, _ = x.shape
    x1, n2t, route, ncs, nps = _mixer(x, mod3, boff, hc, hp, start_pos, mix_w, tl, rc)
    nblk, bstart, tok, wt, p_alloc = _dispatch(route, s_tok)
    n2t3 = n2t.reshape(bsz * seq, ROW_VREGS, LANES)
    y = _moe(nblk, bstart, tok, wt, n2t3, w_gu, w_dn, x1, mod3, boff, g_final, s_tok, tlf, p_alloc)
    return y, ncs[None, :, CONV_PAD - (CONV_WIDTH - 1):, :], nps[None, :, 1:, :]


def kernel(x_prompt, x_sample, state_conv, state_pool, c_prompt, c_sample, w_ada, b_ada, g_mix, w_in, w_dw, b_dw, g_ln, b_ln, w_pw, w_pool, s_pool, w_out, g_ffn, w_rg, b_rg, w_re, b_re, w_gate, w_up, w_down, g_final):
    bp, lp, _ = x_prompt.shape
    bs, ls, _ = x_sample.shape
    past_len = 1024

    c_all = jnp.concatenate([c_prompt, c_sample], axis=0)
    mod = _ada(c_all, w_ada[0], b_ada[0][None, :])
    mod3 = mod[:, None, :]

    w_r = jnp.concatenate(
        [w_rg[0], w_re[0], jnp.zeros((D, N_ROUTE - N_GROUPS - N_EXPERTS), _F32)], axis=1).T.astype(_BF16)
    b_r = jnp.concatenate(
        [b_rg[0], b_re[0], jnp.zeros((N_ROUTE - N_GROUPS - N_EXPERTS,), _F32)])[:, None]
    w_dw_p = jnp.concatenate([w_dw[0], jnp.zeros((1, D), _F32)], axis=0)
    mix_w = (g_mix[0][None, :], w_in[0].astype(_BF16), w_dw_p, b_dw[0][None, :], g_ln[0][None, :],
             b_ln[0][None, :], w_pw[0].astype(_BF16), w_pool[0].astype(_BF16), s_pool[0][None, :],
             w_out[0].astype(_BF16), g_ffn[0][None, :], w_r, b_r)
    w_gu = jnp.concatenate([w_gate[0], w_up[0]], axis=2).astype(_BF16)
    w_dn = w_down[0].astype(_BF16)
    g_fin = g_final[None, :]

    hc_p = jnp.zeros((bp, CONV_PAD, D), _F32)
    hp_p = jnp.zeros((bp, POOL_PAD, D), _F32)
    hc_s = jnp.pad(state_conv[0], ((0, 0), (CONV_PAD - (CONV_WIDTH - 1), 0), (0, 0)))
    hp_s = jnp.pad(state_pool[0], ((0, 0), (1, 0), (0, 0)))

    y_p, ncp, npp = _stream(x_prompt, mod3, 0, hc_p, hp_p, 0, mix_w, w_gu, w_dn, g_fin,
                            tl=512, rc=128, s_tok=4096, tlf=512)
    y_s, ncs, nps = _stream(x_sample, mod3, bp, hc_s, hp_s, past_len, mix_w, w_gu, w_dn, g_fin,
                            tl=ls, rc=ls, s_tok=bs * ls, tlf=ls)
    return (y_p, y_s, ncp, npp, ncs, nps)
```

```python
import functools

import jax
import jax.numpy as jnp
from jax import lax
from jax.experimental import pallas as pl
from jax.experimental.pallas import tpu as pltpu

D = 1024
N_IN = 5
N_ADA = 6
CONV_WIDTH = 31
CONV_PAD = 32
POOL_PAD = 16
HIST_STRIDE = 2
CONV_ROWS = 64
POOL_WINDOWS = (2, 4, 8, 16)
POOL_GROUP_W = D // 4
N_GROUPS = 4
EPG = 8
N_EXPERTS = N_GROUPS * EPG
N_ROUTE = 40
D_EXPERT = D // 2
EPS = 1e-6
LANES = 128
SUBLANES = 8
ROW_VREGS = D // LANES
MOE_ROWS = 144
MOE_STRIDE = MOE_ROWS + 8
MOE_LEAD = 1024
MOE_DUMP = 8
MOE_EXPERTS_PER_STEP = 2
VMEM_LIMIT = 56 * 1024 * 1024

_F32 = jnp.float32
_BF16 = jnp.bfloat16


def _dot(a, b):
    return jnp.dot(a, b, preferred_element_type=_F32)


def _sigmoid(x):
    return jax.nn.sigmoid(x)


def _sigmoid_eup(x):
    return 0.5 * jnp.tanh(0.5 * x) + 0.5


def _round_bf16(x):
    return x.astype(_BF16).astype(_F32)


def _ada_kernel(c_ref, w_ref, b_ref, o_ref):
    c = c_ref[...]
    s = (c * _sigmoid(c)).astype(_BF16)
    o_ref[...] = _dot(s, w_ref[...].astype(_BF16)) + b_ref[...]


def _ada(c_all, w_ada, b_ada):
    nb = c_all.shape[0]
    return pl.pallas_call(
        _ada_kernel,
        grid=(N_ADA,),
        in_specs=[
            pl.BlockSpec((nb, D), lambda j: (0, 0)),
            pl.BlockSpec((D, D), lambda j: (0, j)),
            pl.BlockSpec((1, D), lambda j: (0, j)),
        ],
        out_specs=pl.BlockSpec((nb, D), lambda j: (0, j)),
        out_shape=jax.ShapeDtypeStruct((nb, N_ADA * D), _F32),
        name="ada",
    )(c_all, w_ada, b_ada)


def _mix_kernel(x_ref, mod_ref, hc_ref, hp_ref, gmix_ref, win_ref, w3_ref, wdw_ref, bdw_ref, gln_ref,
                bln_ref, wpw_ref, wpool_ref, spool_ref, wout_ref, gffn_ref, wr_ref, br_ref,
                x1_ref, n2t_ref, route_ref, cnt_ref, ncs_ref, nps_ref,
                nb_scr, vbuf, ubuf, t1, t2, z3, ab_scr, pb_scr, *, tl, rc, start_pos):
    l = pl.program_id(1)
    nchunk = tl // rc
    slab = 3 * D // nchunk

    def chunks(body):
        def step(i, carry):
            body(pl.multiple_of(i * rc, rc))
            return carry
        lax.fori_loop(0, nchunk, step, 0)

    def trow(j, n):
        return pl.ds(HIST_STRIDE * j, n, stride=HIST_STRIDE)

    def hist_put(buf, j, val):
        for lc in range(ROW_VREGS):
            buf[lc, trow(j, val.shape[0]), :] = val[:, lc * LANES:(lc + 1) * LANES]

    def hist_get(buf, j, n):
        return jnp.concatenate([buf[lc, trow(j, n), :] for lc in range(ROW_VREGS)], axis=1)

    @pl.when(l == 0)
    def _():
        hist_put(vbuf, 0, _round_bf16(hc_ref[...]))
        hist_put(ubuf, 0, hp_ref[...])

    sh1 = mod_ref[:, 0 * D:1 * D]
    sc1 = mod_ref[:, 1 * D:2 * D]
    gt1 = mod_ref[:, 2 * D:3 * D]
    sh2 = mod_ref[:, 3 * D:4 * D]
    sc2 = mod_ref[:, 4 * D:5 * D]
    gain1 = gmix_ref[...] * (1.0 + sc1)
    gain2 = gffn_ref[...] * (1.0 + sc2)

    def norm1(r):
        xc = x_ref[pl.ds(r, rc), :]
        ms = jnp.mean(xc * xc, axis=-1, keepdims=True)
        nb_scr[pl.ds(r, rc), :] = (xc * lax.rsqrt(ms + EPS) * gain1 + sh1).astype(_BF16)
    chunks(norm1)

    t1[...] = _dot(nb_scr[...], win_ref[:, 0 * D:1 * D])
    t2[...] = _dot(nb_scr[...], win_ref[:, 1 * D:2 * D])

    def glu(r):
        v = t1[pl.ds(r, rc), :] * _sigmoid_eup(t2[pl.ds(r, rc), :])
        hist_put(vbuf, CONV_PAD + r, _round_bf16(v))
    chunks(glu)
    ncs_ref[...] = t1[tl - CONV_PAD:tl, :] * _sigmoid_eup(t2[tl - CONV_PAD:tl, :])

    def z3_cols(rows, col0, width):
        si, off = divmod(col0, slab)
        assert off + width <= slab
        return z3[si, rows, off:off + width]

    def conv(c, carry):
        r = pl.multiple_of(c * rc, rc)
        for lc in range(ROW_VREGS):
            cs = slice(lc * LANES, (lc + 1) * LANES)
            taps = [jnp.broadcast_to(wdw_ref[k:k + 1, cs], (SUBLANES, LANES)) for k in range(CONV_WIDTH)]
            bias = jnp.broadcast_to(bdw_ref[:, cs], (SUBLANES, LANES))
            ngrp = min(rc, CONV_ROWS) // SUBLANES
            for r2 in range(0, rc, ngrp * SUBLANES):
                accs = [bias] * ngrp
                for o in range((ngrp - 1) * SUBLANES + CONV_WIDTH):
                    xw = vbuf[lc, trow(r + r2 + o + (CONV_PAD - CONV_WIDTH + 1), SUBLANES), :]
                    for i in range(ngrp):
                        k = o - i * SUBLANES
                        if 0 <= k < CONV_WIDTH:
                            accs[i] = accs[i] + taps[k] * xw
                for i in range(ngrp):
                    t1[pl.ds(r + r2 + i * SUBLANES, SUBLANES), cs] = accs[i]
        cv = t1[pl.ds(r, rc), :]
        mu = jnp.mean(cv, axis=-1, keepdims=True)
        xc = cv - mu
        a = xc * lax.rsqrt(jnp.mean(xc * xc, axis=-1, keepdims=True) + EPS) * gln_ref[...] + bln_ref[...]
        ab_scr[pl.ds(r, rc), :] = (a * _sigmoid(a)).astype(_BF16)
        z3[c] = _dot(nb_scr[...], w3_ref[c])
        return carry
    lax.fori_loop(0, nchunk, conv, 0)

    hist_put(ubuf, POOL_PAD, jnp.concatenate(
        [z3_cols(slice(None), q * LANES, LANES) for q in range(ROW_VREGS)], axis=1))
    t1[...] = _dot(ab_scr[...], wpw_ref[...])

    for c in range(nchunk):
        r = c * rc
        pos = start_pos + l * tl + r + lax.broadcasted_iota(jnp.int32, (rc, 1), 0)
        for lc in range(ROW_VREGS):
            cs = slice(lc * LANES, (lc + 1) * LANES)
            w = POOL_WINDOWS[lc * LANES // POOL_GROUP_W]
            u = z3_cols(slice(r, r + rc), lc * LANES, LANES)
            sw = u
            for i in range(1, w):
                sw = sw + ubuf[lc, trow(POOL_PAD + r - i, rc), :]
            cnt = jnp.minimum(pos + 1, w).astype(_F32)
            pb_scr[r:r + rc, cs] = (sw * (1.0 / cnt) - u).astype(_BF16)

    for g in range(N_GROUPS):
        gs = slice(g * POOL_GROUP_W, (g + 1) * POOL_GROUP_W)
        t2[:, gs] = _dot(pb_scr[:, gs], wpool_ref[g]) * spool_ref[:, gs]

    def mix(r):
        rs = pl.ds(r, rc)
        for q in range(D // POOL_GROUP_W):
            qs = slice(q * POOL_GROUP_W, (q + 1) * POOL_GROUP_W)
            ga = z3_cols(rs, D + q * POOL_GROUP_W, POOL_GROUP_W)
            gb = z3_cols(rs, 2 * D + q * POOL_GROUP_W, POOL_GROUP_W)
            ab_scr[rs, qs] = (_sigmoid_eup(ga) * t1[rs, qs] + _sigmoid_eup(gb) * t2[rs, qs]).astype(_BF16)
    chunks(mix)

    t1[...] = _dot(ab_scr[...], wout_ref[...])

    def resid(r):
        rs = pl.ds(r, rc)
        x1 = x_ref[rs, :] + gt1 * t1[rs, :]
        x1_ref[rs, :] = x1
        ms = jnp.mean(x1 * x1, axis=-1, keepdims=True)
        n2 = x1 * lax.rsqrt(ms + EPS) * gain2 + sh2
        nb_scr[rs, :] = n2.astype(_BF16)
        for j in range(ROW_VREGS):
            n2t_ref[pl.ds(r * ROW_VREGS + j, rc, stride=ROW_VREGS), :] = n2[:, j * LANES:(j + 1) * LANES]
    chunks(resid)

    lg = lax.dot_general(wr_ref[...], nb_scr[...], (((1,), (1,)), ((), ())),
                         preferred_element_type=_F32) + br_ref[...]
    lgg = lg[0:N_GROUPS, :]
    mg = jnp.max(lgg, axis=0, keepdims=True)
    p_g = 1.0 / jnp.sum(jnp.exp(lgg - mg), axis=0, keepdims=True)
    gi = lax.broadcasted_iota(jnp.int32, lgg.shape, 0)
    g_idx = jnp.min(jnp.where(lgg == mg, gi, N_GROUPS), axis=0, keepdims=True)
    sel = jnp.zeros((EPG, tl), _F32)
    for g in range(N_GROUPS):
        sel = jnp.where(g_idx == g, lg[N_GROUPS + g * EPG:N_GROUPS + (g + 1) * EPG, :], sel)
    ms = jnp.max(sel, axis=0, keepdims=True)
    es = jnp.exp(sel - ms)
    pe = es / jnp.sum(es, axis=0, keepdims=True)
    ei = lax.broadcasted_iota(jnp.int32, pe.shape, 0)
    v1 = jnp.max(pe, axis=0, keepdims=True)
    i1 = jnp.min(jnp.where(pe == v1, ei, EPG), axis=0, keepdims=True)
    pe2 = jnp.where(ei == i1, -1.0, pe)
    v2 = jnp.max(pe2, axis=0, keepdims=True)
    i2 = jnp.min(jnp.where(pe2 == v2, ei, EPG), axis=0, keepdims=True)
    scale = p_g / (v1 + v2)
    route_ref[0:1, :] = (g_idx * EPG + i1).astype(_F32)
    route_ref[1:2, :] = (g_idx * EPG + i2).astype(_F32)
    route_ref[2:3, :] = v1 * scale
    route_ref[3:4, :] = v2 * scale
    route_ref[4:8, :] = jnp.zeros((4, tl), _F32)

    xi = lax.broadcasted_iota(jnp.int32, (N_EXPERTS, tl), 0)
    hit = ((xi == g_idx * EPG + i1).astype(_F32) + (xi == g_idx * EPG + i2).astype(_F32))
    if tl % LANES == 0:
        cnt_ref[...] = sum(hit[:, q * LANES:(q + 1) * LANES] for q in range(tl // LANES))
    else:
        cnt_ref[...] = jnp.zeros((N_EXPERTS, LANES), _F32)
        cnt_ref[:, 0:tl] = hit

    utail = hist_get(ubuf, tl, POOL_PAD)
    nps_ref[...] = utail
    hist_put(vbuf, 0, hist_get(vbuf, tl, CONV_PAD))
    hist_put(ubuf, 0, utail)


def _const_spec(shape):
    nd = len(shape)
    return pl.BlockSpec(shape, lambda b, l: (0,) * nd, pipeline_mode=pl.Buffered(1))


def _mixer(x, mod3, boff, hc, hp, start_pos, wts, tl, rc):
    bsz, seq, _ = x.shape
    nl = seq // tl
    nchunk = tl // rc
    slab = 3 * D // nchunk
    w_in = wts[1]
    w3 = jnp.transpose(w_in[:, 2 * D:].reshape(D, nchunk, slab), (1, 0, 2))
    wts = (wts[0], w_in[:, :2 * D], w3) + tuple(wts[2:])
    kern = functools.partial(_mix_kernel, tl=tl, rc=rc, start_pos=start_pos)
    in_specs = [
        pl.BlockSpec((None, tl, D), lambda b, l: (b, l, 0)),
        pl.BlockSpec((None, 1, N_ADA * D), lambda b, l: (b + boff, 0, 0)),
        pl.BlockSpec((None, CONV_PAD, D), lambda b, l: (b, 0, 0)),
        pl.BlockSpec((None, POOL_PAD, D), lambda b, l: (b, 0, 0)),
    ] + [_const_spec(w.shape) for w in wts]
    out_specs = [
        pl.BlockSpec((None, tl, D), lambda b, l: (b, l, 0)),
        pl.BlockSpec((tl * ROW_VREGS, LANES), lambda b, l: (b * nl + l, 0)),
        pl.BlockSpec((None, SUBLANES, tl), lambda b, l: (b, 0, l)),
        pl.BlockSpec((None, N_EXPERTS, LANES), lambda b, l: (b * nl + l, 0, 0)),
        pl.BlockSpec((None, CONV_PAD, D), lambda b, l: (b, 0, 0)),
        pl.BlockSpec((None, POOL_PAD, D), lambda b, l: (b, 0, 0)),
    ]
    out_shape = [
        jax.ShapeDtypeStruct((bsz, seq, D), _F32),
        jax.ShapeDtypeStruct((bsz * seq * ROW_VREGS, LANES), _F32),
        jax.ShapeDtypeStruct((bsz, SUBLANES, seq), _F32),
        jax.ShapeDtypeStruct((bsz * nl, N_EXPERTS, LANES), _F32),
        jax.ShapeDtypeStruct((bsz, CONV_PAD, D), _F32),
        jax.ShapeDtypeStruct((bsz, POOL_PAD, D), _F32),
    ]
    scratch = [
        pltpu.VMEM((tl, D), _BF16),
        pltpu.VMEM((ROW_VREGS, HIST_STRIDE * (tl + CONV_PAD), LANES), _F32),
        pltpu.VMEM((ROW_VREGS, HIST_STRIDE * (tl + POOL_PAD), LANES), _F32),
        pltpu.VMEM((tl, D), _F32),
        pltpu.VMEM((tl, D), _F32),
        pltpu.VMEM((nchunk, tl, slab), _F32),
        pltpu.VMEM((tl, D), _BF16),
        pltpu.VMEM((tl, D), _BF16),
    ]
    return pl.pallas_call(
        kern,
        grid=(bsz, nl),
        in_specs=in_specs,
        out_specs=out_specs,
        out_shape=out_shape,
        scratch_shapes=scratch,
        compiler_params=pltpu.CompilerParams(
            dimension_semantics=("arbitrary", "arbitrary"), vmem_limit_bytes=VMEM_LIMIT),
        name="mixer",
    )(x, mod3, hc, hp, *wts)


def _moe_kernel(nblk_ref, bstart_ref, tok_ref, wt_ref, src_ref, wgu_ref, wdn_ref, x1_ref, mod_ref, g_ref,
                y_ref, acc, xc_scr, yc_scr, xb_scr, *, s_tok, tlf, unroll):
    s = pl.program_id(0)
    e = pl.program_id(1)
    n_esteps = N_EXPERTS // MOE_EXPERTS_PER_STEP

    def arow(t):
        return pl.ds(pl.multiple_of(t * ROW_VREGS, ROW_VREGS), ROW_VREGS)

    def gather(blk):
        base = MOE_LEAD + blk * MOE_ROWS
        for mi in range(MOE_ROWS):
            t = jnp.minimum(tok_ref[base + mi], s_tok - 1)
            xc_scr[pl.ds(mi, ROW_VREGS, stride=MOE_STRIDE), :] = src_ref[t]

    def scatter(blk):
        base = MOE_LEAD + blk * MOE_ROWS
        for g0 in range(0, MOE_ROWS, unroll):
            ts = [tok_ref[base + g0 + i] for i in range(unroll)]
            vals = [acc[arow(ts[i]), :] + wt_ref[base + g0 + i]
                    * yc_scr[pl.ds(g0 + i, ROW_VREGS, stride=MOE_STRIDE), :] for i in range(unroll)]
            for i in range(unroll):
                acc[arow(ts[i]), :] = vals[i]

    @pl.when(e == 0)
    def _():
        acc[...] = jnp.zeros(acc.shape, _F32)
        yc_scr[...] = jnp.zeros(yc_scr.shape, _F32)
        gather(0)

    def block(ee, blk):
        xb_scr[...] = jnp.concatenate(
            [xc_scr[pl.ds(j * MOE_STRIDE, MOE_ROWS), :] for j in range(ROW_VREGS)], axis=1).astype(_BF16)
        gather(blk + 1)
        scatter(blk - 1)
        y = None
        for hs in (slice(0, D_EXPERT // 2), slice(D_EXPERT // 2, D_EXPERT)):
            g = _dot(xb_scr[...], wgu_ref[ee, :, hs])
            u = _dot(xb_scr[...], wgu_ref[ee, :, D_EXPERT + hs.start:D_EXPERT + hs.stop])
            act = (g * _sigmoid(g) * u).astype(_BF16)
            yh = _dot(act, wdn_ref[ee, hs, :])
            y = yh if y is None else y + yh
        for j in range(ROW_VREGS):
            yc_scr[pl.ds(j * MOE_STRIDE, MOE_ROWS), :] = y[:, j * LANES:(j + 1) * LANES]

    for ee in range(MOE_EXPERTS_PER_STEP):
        idx = s * N_EXPERTS + jnp.minimum(e, n_esteps - 1) * MOE_EXPERTS_PER_STEP + ee
        nb = jnp.where(e < n_esteps, nblk_ref[idx], 0)
        b0 = bstart_ref[idx]

        def body(b, carry, ee=ee, b0=b0):
            block(ee, b0 + b)
            return carry
        lax.fori_loop(0, nb, body, 0)

    @pl.when(e == n_esteps - 1)
    def _():
        scatter(b0 + nb - 1)

    @pl.when(e >= n_esteps)
    def _():
        row0 = (e - n_esteps) * tlf
        rcf = min(tlf, 128)

        def body(i, carry):
            r = pl.multiple_of(i * rcf, rcf)
            f = jnp.concatenate(
                [acc[pl.ds((row0 + r) * ROW_VREGS + j, rcf, stride=ROW_VREGS), :] for j in range(ROW_VREGS)],
                axis=1)
            x2 = x1_ref[pl.ds(r, rcf), :] + mod_ref[...] * f
            ms = jnp.mean(x2 * x2, axis=-1, keepdims=True)
            y_ref[pl.ds(r, rcf), :] = x2 * lax.rsqrt(ms + EPS) * g_ref[...]
            return carry
        lax.fori_loop(0, tlf // rcf, body, 0)


def _moe(nblk, bstart, tok, wt, n2t, w_gu, w_dn, x1, mod3, boff, g_final, s_tok, tlf, p_max):
    bsz, seq, _ = x1.shape
    n_tok = bsz * seq
    n_super = n_tok // s_tok
    n_esteps = N_EXPERTS // MOE_EXPERTS_PER_STEP
    per = s_tok // tlf
    kern = functools.partial(_moe_kernel, s_tok=s_tok, tlf=tlf, unroll=8)

    def estep(e):
        return jnp.minimum(e, n_esteps - 1)

    def ostep(s, e):
        return s * per + jnp.maximum(e - n_esteps, 0)

    grid_spec = pltpu.PrefetchScalarGridSpec(
        num_scalar_prefetch=2,
        grid=(n_super, n_esteps + per),
        in_specs=[
            pl.BlockSpec((p_max,), lambda s, e, *_: (s,), memory_space=pltpu.SMEM),
            pl.BlockSpec((p_max,), lambda s, e, *_: (s,), memory_space=pltpu.SMEM),
            pl.BlockSpec((s_tok, ROW_VREGS, LANES), lambda s, e, *_: (s, 0, 0),
                         pipeline_mode=pl.Buffered(1)),
            pl.BlockSpec((MOE_EXPERTS_PER_STEP, D, D), lambda s, e, *_: (estep(e), 0, 0)),
            pl.BlockSpec((MOE_EXPERTS_PER_STEP, D_EXPERT, D), lambda s, e, *_: (estep(e), 0, 0)),
            pl.BlockSpec((tlf, D), lambda s, e, *_: (ostep(s, e), 0)),
            pl.BlockSpec((None, 1, D), lambda s, e, *_: (ostep(s, e) * tlf // seq + boff, 0, N_ADA - 1)),
            pl.BlockSpec((1, D), lambda s, e, *_: (0, 0)),
        ],
        out_specs=pl.BlockSpec((tlf, D), lambda s, e, *_: (ostep(s, e), 0)),
        scratch_shapes=[
            pltpu.VMEM(((s_tok + MOE_DUMP) * ROW_VREGS, LANES), _F32),
            pltpu.VMEM((ROW_VREGS * MOE_STRIDE, LANES), _F32),
            pltpu.VMEM((ROW_VREGS * MOE_STRIDE, LANES), _F32),
            pltpu.VMEM((MOE_ROWS, D), _BF16),
        ],
    )
    y = pl.pallas_call(
        kern,
        grid_spec=grid_spec,
        out_shape=jax.ShapeDtypeStruct((n_tok, D), _F32),
        compiler_params=pltpu.CompilerParams(
            dimension_semantics=("arbitrary", "arbitrary"), vmem_limit_bytes=VMEM_LIMIT),
        name="moe",
    )(nblk, bstart, tok, wt, n2t, w_gu, w_dn, x1.reshape(n_tok, D), mod3, g_final)
    return y.reshape(bsz, seq, D)


def _dispatch(route, cnt, s_tok):
    bsz, _, seq = route.shape
    n_tok = bsz * seq
    n_super = n_tok // s_tok
    r = jnp.transpose(route[:, 0:4, :], (1, 0, 2)).reshape(4, n_super, s_tok)
    e_real = jnp.concatenate([r[0], r[1]], axis=1).astype(jnp.int32)
    w_real = jnp.concatenate([r[2], r[3]], axis=1)
    t_real = jnp.broadcast_to(jnp.tile(jnp.arange(s_tok, dtype=jnp.int32), 2)[None, :], e_real.shape)
    experts = jnp.arange(N_EXPERTS, dtype=jnp.int32)
    counts = jnp.sum(cnt.reshape(n_super, -1, N_EXPERTS, LANES), axis=(1, 3)).astype(jnp.int32)
    n_pad = (-counts) % MOE_ROWS
    cand = jnp.arange(MOE_ROWS - 1, dtype=jnp.int32)
    key_pad = jnp.where(cand[None, None, :] < n_pad[:, :, None], experts[None, :, None], N_EXPERTS)
    key_pad = key_pad.reshape(n_super, N_EXPERTS * (MOE_ROWS - 1)).astype(jnp.int32)
    n_sort = 2 * s_tok + N_EXPERTS * (MOE_ROWS - 1)
    p_alloc = (n_sort + MOE_ROWS + 1023) // 1024 * 1024
    tok_bits = s_tok.bit_length()
    keys = jnp.concatenate([(e_real << tok_bits) | t_real, (key_pad << tok_bits) | s_tok], axis=1)
    wts = jnp.concatenate([w_real, jnp.zeros((n_super, n_sort - 2 * s_tok), _F32)], axis=1)
    keys, wt = lax.sort((keys, wts), dimension=1, num_keys=1)
    tok = keys & ((1 << tok_bits) - 1)
    tok = jnp.pad(tok, ((0, 0), (MOE_LEAD, p_alloc - n_sort)), constant_values=s_tok)
    wt = jnp.pad(wt, ((0, 0), (MOE_LEAD, p_alloc - n_sort)))
    p_alloc += MOE_LEAD
    pcounts = counts + n_pad
    pstarts = jnp.cumsum(pcounts, axis=1) - pcounts
    nblk = (pcounts // MOE_ROWS).reshape(-1)
    bstart = (pstarts // MOE_ROWS).reshape(-1)
    return nblk, bstart, tok.reshape(-1), wt.reshape(-1), p_alloc


def _stream(x, mod3, boff, hc, hp, start_pos, mix_w, w_gu, w_dn, g_final, tl, rc, s_tok, tlf):
    bsz, seq, _ = x.shape
    x1, n2t, route, cnt, ncs, nps = _mixer(x, mod3, boff, hc, hp, start_pos, mix_w, tl, rc)
    nblk, bstart, tok, wt, p_alloc = _dispatch(route, cnt, s_tok)
    n2t3 = n2t.reshape(bsz * seq, ROW_VREGS, LANES)
    y = _moe(nblk, bstart, tok, wt, n2t3, w_gu, w_dn, x1, mod3, boff, g_final, s_tok, tlf, p_alloc)
    return y, ncs[None, :, CONV_PAD - (CONV_WIDTH - 1):, :], nps[None, :, 1:, :]


def kernel(x_prompt, x_sample, state_conv, state_pool, c_prompt, c_sample, w_ada, b_ada, g_mix, w_in, w_dw, b_dw, g_ln, b_ln, w_pw, w_pool, s_pool, w_out, g_ffn, w_rg, b_rg, w_re, b_re, w_gate, w_up, w_down, g_final):
    bp, lp, _ = x_prompt.shape
    bs, ls, _ = x_sample.shape
    past_len = 1024

    c_all = jnp.concatenate([c_prompt, c_sample], axis=0)
    mod = _ada(c_all, w_ada[0], b_ada[0][None, :])
    mod3 = mod[:, None, :]

    w_r = jnp.concatenate(
        [w_rg[0], w_re[0], jnp.zeros((D, N_ROUTE - N_GROUPS - N_EXPERTS), _F32)], axis=1).T.astype(_BF16)
    b_r = jnp.concatenate(
        [b_rg[0], b_re[0], jnp.zeros((N_ROUTE - N_GROUPS - N_EXPERTS,), _F32)])[:, None]
    w_dw_p = jnp.concatenate([w_dw[0], jnp.zeros((1, D), _F32)], axis=0)
    mix_w = (g_mix[0][None, :], w_in[0].astype(_BF16), w_dw_p, b_dw[0][None, :], g_ln[0][None, :],
             b_ln[0][None, :], w_pw[0].astype(_BF16), w_pool[0].astype(_BF16), s_pool[0][None, :],
             w_out[0].astype(_BF16), g_ffn[0][None, :], w_r, b_r)
    w_gu = jnp.concatenate([w_gate[0], w_up[0]], axis=2).astype(_BF16)
    w_dn = w_down[0].astype(_BF16)
    g_fin = g_final[None, :]

    hc_p = jnp.zeros((bp, CONV_PAD, D), _F32)
    hp_p = jnp.zeros((bp, POOL_PAD, D), _F32)
    hc_s = jnp.pad(state_conv[0], ((0, 0), (CONV_PAD - (CONV_WIDTH - 1), 0), (0, 0)))
    hp_s = jnp.pad(state_pool[0], ((0, 0), (1, 0), (0, 0)))

    y_p, ncp, npp = _stream(x_prompt, mod3, 0, hc_p, hp_p, 0, mix_w, w_gu, w_dn, g_fin,
                            tl=512, rc=128, s_tok=4096, tlf=512)
    y_s, ncs, nps = _stream(x_sample, mod3, bp, hc_s, hp_s, past_len, mix_w, w_gu, w_dn, g_fin,
                            tl=ls, rc=ls, s_tok=bs * ls, tlf=ls)
    return (y_p, y_s, ncp, npp, ncs, nps)
```

```python
import functools

import jax
import jax.numpy as jnp
from jax import lax
from jax.experimental import pallas as pl
from jax.experimental.pallas import tpu as pltpu

D = 1024
N_IN = 5
N_ADA = 6
CONV_WIDTH = 31
CONV_PAD = 32
POOL_PAD = 16
HIST_STRIDE = 2
CONV_ROWS = 64
POOL_WINDOWS = (2, 4, 8, 16)
POOL_GROUP_W = D // 4
N_GROUPS = 4
EPG = 8
N_EXPERTS = N_GROUPS * EPG
N_ROUTE = 40
D_EXPERT = D // 2
EPS = 1e-6
LANES = 128
SUBLANES = 8
ROW_VREGS = D // LANES
MOE_ROWS = 144
MOE_STRIDE = MOE_ROWS + 8
MOE_LEAD = 1024
MOE_DUMP = 8
MOE_EXPERTS_PER_STEP = 2
VMEM_LIMIT = 56 * 1024 * 1024

_F32 = jnp.float32
_BF16 = jnp.bfloat16


def _dot(a, b):
    return jnp.dot(a, b, preferred_element_type=_F32)


def _sigmoid(x):
    return jax.nn.sigmoid(x)


def _sigmoid_eup(x):
    return 0.5 * jnp.tanh(0.5 * x) + 0.5


def _round_bf16(x):
    return x.astype(_BF16).astype(_F32)


def _ada_kernel(c_ref, w_ref, b_ref, o_ref):
    c = c_ref[...]
    s = (c * _sigmoid(c)).astype(_BF16)
    o_ref[...] = _dot(s, w_ref[...].astype(_BF16)) + b_ref[...]


def _ada(c_all, w_ada, b_ada):
    nb = c_all.shape[0]
    return pl.pallas_call(
        _ada_kernel,
        grid=(N_ADA,),
        in_specs=[
            pl.BlockSpec((nb, D), lambda j: (0, 0)),
            pl.BlockSpec((D, D), lambda j: (0, j)),
            pl.BlockSpec((1, D), lambda j: (0, j)),
        ],
        out_specs=pl.BlockSpec((nb, D), lambda j: (0, j)),
        out_shape=jax.ShapeDtypeStruct((nb, N_ADA * D), _F32),
        name="ada",
    )(c_all, w_ada, b_ada)


def _mix_kernel(x_ref, mod_ref, hc_ref, hp_ref, gmix_ref, win_ref, w3_ref, wdw_ref, bdw_ref, gln_ref,
                bln_ref, wpw_ref, wpool_ref, spool_ref, wout_ref, gffn_ref, wr_ref, br_ref,
                x1_ref, n2t_ref, route_ref, cnt_ref, ncs_ref, nps_ref,
                nb_scr, vbuf, ubuf, t1, t2, z3, ab_scr, pb_scr, *, tl, rc, start_pos):
    l = pl.program_id(1)
    nchunk = tl // rc
    slab = 3 * D // nchunk

    def chunks(body):
        def step(i, carry):
            body(pl.multiple_of(i * rc, rc))
            return carry
        lax.fori_loop(0, nchunk, step, 0)

    def trow(j, n):
        return pl.ds(HIST_STRIDE * j, n, stride=HIST_STRIDE)

    def hist_put(buf, j, val):
        for lc in range(ROW_VREGS):
            buf[lc, trow(j, val.shape[0]), :] = val[:, lc * LANES:(lc + 1) * LANES]

    def hist_get(buf, j, n):
        return jnp.concatenate([buf[lc, trow(j, n), :] for lc in range(ROW_VREGS)], axis=1)

    @pl.when(l == 0)
    def _():
        hist_put(vbuf, 0, _round_bf16(hc_ref[...]))
        hist_put(ubuf, 0, hp_ref[...])

    sh1 = mod_ref[:, 0 * D:1 * D]
    sc1 = mod_ref[:, 1 * D:2 * D]
    gt1 = mod_ref[:, 2 * D:3 * D]
    sh2 = mod_ref[:, 3 * D:4 * D]
    sc2 = mod_ref[:, 4 * D:5 * D]
    gain1 = gmix_ref[...] * (1.0 + sc1)
    gain2 = gffn_ref[...] * (1.0 + sc2)

    def norm1(r):
        xc = x_ref[pl.ds(r, rc), :]
        ms = jnp.mean(xc * xc, axis=-1, keepdims=True)
        nb_scr[pl.ds(r, rc), :] = (xc * lax.rsqrt(ms + EPS) * gain1 + sh1).astype(_BF16)
    chunks(norm1)

    t1[...] = _dot(nb_scr[...], win_ref[:, 0 * D:1 * D])
    t2[...] = _dot(nb_scr[...], win_ref[:, 1 * D:2 * D])

    def glu(r):
        v = t1[pl.ds(r, rc), :] * _sigmoid_eup(t2[pl.ds(r, rc), :])
        hist_put(vbuf, CONV_PAD + r, _round_bf16(v))
    chunks(glu)
    ncs_ref[...] = t1[tl - CONV_PAD:tl, :] * _sigmoid_eup(t2[tl - CONV_PAD:tl, :])

    def z3_cols(rows, col0, width):
        si, off = divmod(col0, slab)
        assert off + width <= slab
        return z3[si, rows, off:off + width]

    def conv(c, carry):
        r = pl.multiple_of(c * rc, rc)
        for lc in range(ROW_VREGS):
            cs = slice(lc * LANES, (lc + 1) * LANES)
            taps = [jnp.broadcast_to(wdw_ref[k:k + 1, cs], (SUBLANES, LANES)) for k in range(CONV_WIDTH)]
            bias = jnp.broadcast_to(bdw_ref[:, cs], (SUBLANES, LANES))
            ngrp = min(rc, CONV_ROWS) // SUBLANES
            for r2 in range(0, rc, ngrp * SUBLANES):
                accs = [bias] * ngrp
                for o in range((ngrp - 1) * SUBLANES + CONV_WIDTH):
                    xw = vbuf[lc, trow(r + r2 + o + (CONV_PAD - CONV_WIDTH + 1), SUBLANES), :]
                    for i in range(ngrp):
                        k = o - i * SUBLANES
                        if 0 <= k < CONV_WIDTH:
                            accs[i] = accs[i] + taps[k] * xw
                for i in range(ngrp):
                    t1[pl.ds(r + r2 + i * SUBLANES, SUBLANES), cs] = accs[i]
        cv = t1[pl.ds(r, rc), :]
        mu = jnp.mean(cv, axis=-1, keepdims=True)
        xc = cv - mu
        a = xc * lax.rsqrt(jnp.mean(xc * xc, axis=-1, keepdims=True) + EPS) * gln_ref[...] + bln_ref[...]
        ab_scr[pl.ds(r, rc), :] = (a * _sigmoid(a)).astype(_BF16)
        z3[c] = _dot(nb_scr[...], w3_ref[c])
        return carry
    lax.fori_loop(0, nchunk, conv, 0)

    hist_put(ubuf, POOL_PAD, jnp.concatenate(
        [z3_cols(slice(None), q * LANES, LANES) for q in range(ROW_VREGS)], axis=1))
    t1[...] = _dot(ab_scr[...], wpw_ref[...])

    for c in range(nchunk):
        r = c * rc
        pos = start_pos + l * tl + r + lax.broadcasted_iota(jnp.int32, (rc, 1), 0)
        for lc in range(ROW_VREGS):
            cs = slice(lc * LANES, (lc + 1) * LANES)
            w = POOL_WINDOWS[lc * LANES // POOL_GROUP_W]
            u = z3_cols(slice(r, r + rc), lc * LANES, LANES)
            sw = u
            for i in range(1, w):
                sw = sw + ubuf[lc, trow(POOL_PAD + r - i, rc), :]
            cnt = jnp.minimum(pos + 1, w).astype(_F32)
            pb_scr[r:r + rc, cs] = (sw * (1.0 / cnt) - u).astype(_BF16)

    for g in range(N_GROUPS):
        gs = slice(g * POOL_GROUP_W, (g + 1) * POOL_GROUP_W)
        t2[:, gs] = _dot(pb_scr[:, gs], wpool_ref[g]) * spool_ref[:, gs]

    def mix(r):
        rs = pl.ds(r, rc)
        for q in range(D // POOL_GROUP_W):
            qs = slice(q * POOL_GROUP_W, (q + 1) * POOL_GROUP_W)
            ga = z3_cols(rs, D + q * POOL_GROUP_W, POOL_GROUP_W)
            gb = z3_cols(rs, 2 * D + q * POOL_GROUP_W, POOL_GROUP_W)
            ab_scr[rs, qs] = (_sigmoid_eup(ga) * t1[rs, qs] + _sigmoid_eup(gb) * t2[rs, qs]).astype(_BF16)
    chunks(mix)

    t1[...] = _dot(ab_scr[...], wout_ref[...])

    def resid(r):
        rs = pl.ds(r, rc)
        x1 = x_ref[rs, :] + gt1 * t1[rs, :]
        x1_ref[rs, :] = x1
        ms = jnp.mean(x1 * x1, axis=-1, keepdims=True)
        n2 = x1 * lax.rsqrt(ms + EPS) * gain2 + sh2
        nb_scr[rs, :] = n2.astype(_BF16)
        for j in range(ROW_VREGS):
            n2t_ref[pl.ds(r * ROW_VREGS + j, rc, stride=ROW_VREGS), :] = n2[:, j * LANES:(j + 1) * LANES]
    chunks(resid)

    lg = lax.dot_general(wr_ref[...], nb_scr[...], (((1,), (1,)), ((), ())),
                         preferred_element_type=_F32) + br_ref[...]
    lgg = lg[0:N_GROUPS, :]
    mg = jnp.max(lgg, axis=0, keepdims=True)
    p_g = 1.0 / jnp.sum(jnp.exp(lgg - mg), axis=0, keepdims=True)
    gi = lax.broadcasted_iota(jnp.int32, lgg.shape, 0)
    g_idx = jnp.min(jnp.where(lgg == mg, gi, N_GROUPS), axis=0, keepdims=True)
    sel = jnp.zeros((EPG, tl), _F32)
    for g in range(N_GROUPS):
        sel = jnp.where(g_idx == g, lg[N_GROUPS + g * EPG:N_GROUPS + (g + 1) * EPG, :], sel)
    ms = jnp.max(sel, axis=0, keepdims=True)
    es = jnp.exp(sel - ms)
    pe = es / jnp.sum(es, axis=0, keepdims=True)
    ei = lax.broadcasted_iota(jnp.int32, pe.shape, 0)
    v1 = jnp.max(pe, axis=0, keepdims=True)
    i1 = jnp.min(jnp.where(pe == v1, ei, EPG), axis=0, keepdims=True)
    pe2 = jnp.where(ei == i1, -1.0, pe)
    v2 = jnp.max(pe2, axis=0, keepdims=True)
    i2 = jnp.min(jnp.where(pe2 == v2, ei, EPG), axis=0, keepdims=True)
    scale = p_g / (v1 + v2)
    route_ref[0:1, :] = (g_idx * EPG + i1).astype(_F32)
    route_ref[1:2, :] = (g_idx * EPG + i2).astype(_F32)
    route_ref[2:3, :] = v1 * scale
    route_ref[3:4, :] = v2 * scale
    route_ref[4:8, :] = jnp.zeros((4, tl), _F32)

    xi = lax.broadcasted_iota(jnp.int32, (N_EXPERTS, tl), 0)
    hit = ((xi == g_idx * EPG + i1).astype(_F32) + (xi == g_idx * EPG + i2).astype(_F32))
    if tl % LANES == 0:
        cnt_ref[...] = sum(hit[:, q * LANES:(q + 1) * LANES] for q in range(tl // LANES))
    else:
        cnt_ref[...] = jnp.zeros((N_EXPERTS, LANES), _F32)
        cnt_ref[:, 0:tl] = hit

    utail = hist_get(ubuf, tl, POOL_PAD)
    nps_ref[...] = utail
    hist_put(vbuf, 0, hist_get(vbuf, tl, CONV_PAD))
    hist_put(ubuf, 0, utail)


def _const_spec(shape):
    nd = len(shape)
    return pl.BlockSpec(shape, lambda b, l: (0,) * nd, pipeline_mode=pl.Buffered(1))


def _mixer(x, mod3, boff, hc, hp, start_pos, wts, tl, rc):
    bsz, seq, _ = x.shape
    nl = seq // tl
    nchunk = tl // rc
    slab = 3 * D // nchunk
    w_in = wts[1]
    w3 = jnp.transpose(w_in[:, 2 * D:].reshape(D, nchunk, slab), (1, 0, 2))
    wts = (wts[0], w_in[:, :2 * D], w3) + tuple(wts[2:])
    kern = functools.partial(_mix_kernel, tl=tl, rc=rc, start_pos=start_pos)
    in_specs = [
        pl.BlockSpec((None, tl, D), lambda b, l: (b, l, 0)),
        pl.BlockSpec((None, 1, N_ADA * D), lambda b, l: (b + boff, 0, 0)),
        pl.BlockSpec((None, CONV_PAD, D), lambda b, l: (b, 0, 0)),
        pl.BlockSpec((None, POOL_PAD, D), lambda b, l: (b, 0, 0)),
    ] + [_const_spec(w.shape) for w in wts]
    out_specs = [
        pl.BlockSpec((None, tl, D), lambda b, l: (b, l, 0)),
        pl.BlockSpec((tl * ROW_VREGS, LANES), lambda b, l: (b * nl + l, 0)),
        pl.BlockSpec((None, SUBLANES, tl), lambda b, l: (b, 0, l)),
        pl.BlockSpec((None, N_EXPERTS, LANES), lambda b, l: (b * nl + l, 0, 0)),
        pl.BlockSpec((None, CONV_PAD, D), lambda b, l: (b, 0, 0)),
        pl.BlockSpec((None, POOL_PAD, D), lambda b, l: (b, 0, 0)),
    ]
    out_shape = [
        jax.ShapeDtypeStruct((bsz, seq, D), _F32),
        jax.ShapeDtypeStruct((bsz * seq * ROW_VREGS, LANES), _F32),
        jax.ShapeDtypeStruct((bsz, SUBLANES, seq), _F32),
        jax.ShapeDtypeStruct((bsz * nl, N_EXPERTS, LANES), _F32),
        jax.ShapeDtypeStruct((bsz, CONV_PAD, D), _F32),
        jax.ShapeDtypeStruct((bsz, POOL_PAD, D), _F32),
    ]
    scratch = [
        pltpu.VMEM((tl, D), _BF16),
        pltpu.VMEM((ROW_VREGS, HIST_STRIDE * (tl + CONV_PAD), LANES), _F32),
        pltpu.VMEM((ROW_VREGS, HIST_STRIDE * (tl + POOL_PAD), LANES), _F32),
        pltpu.VMEM((tl, D), _F32),
        pltpu.VMEM((tl, D), _F32),
        pltpu.VMEM((nchunk, tl, slab), _F32),
        pltpu.VMEM((tl, D), _BF16),
        pltpu.VMEM((tl, D), _BF16),
    ]
    return pl.pallas_call(
        kern,
        grid=(bsz, nl),
        in_specs=in_specs,
        out_specs=out_specs,
        out_shape=out_shape,
        scratch_shapes=scratch,
        compiler_params=pltpu.CompilerParams(
            dimension_semantics=("arbitrary", "arbitrary"), vmem_limit_bytes=VMEM_LIMIT),
        name="mixer",
    )(x, mod3, hc, hp, *wts)


def _moe_kernel(nblk_ref, bstart_ref, tok_ref, wt_ref, src_ref, wg_ref, wu_ref, wdn_ref, x1_ref, mod_ref, g_ref,
                y_ref, acc, xc_scr, yc_scr, xb_scr, *, s_tok, tlf, unroll):
    s = pl.program_id(0)
    e = pl.program_id(1)
    n_esteps = N_EXPERTS // MOE_EXPERTS_PER_STEP

    def arow(t):
        return pl.ds(pl.multiple_of(t * ROW_VREGS, ROW_VREGS), ROW_VREGS)

    def gather(blk):
        base = MOE_LEAD + blk * MOE_ROWS
        for mi in range(MOE_ROWS):
            t = jnp.minimum(tok_ref[base + mi], s_tok - 1)
            xc_scr[pl.ds(mi, ROW_VREGS, stride=MOE_STRIDE), :] = src_ref[t]

    def scatter(blk):
        base = MOE_LEAD + blk * MOE_ROWS
        for g0 in range(0, MOE_ROWS, unroll):
            ts = [tok_ref[base + g0 + i] for i in range(unroll)]
            vals = [acc[arow(ts[i]), :] + wt_ref[base + g0 + i]
                    * yc_scr[pl.ds(g0 + i, ROW_VREGS, stride=MOE_STRIDE), :] for i in range(unroll)]
            for i in range(unroll):
                acc[arow(ts[i]), :] = vals[i]

    @pl.when(e == 0)
    def _():
        acc[...] = jnp.zeros(acc.shape, _F32)
        yc_scr[...] = jnp.zeros(yc_scr.shape, _F32)
        gather(0)

    def block(ee, blk):
        xb_scr[...] = jnp.concatenate(
            [xc_scr[pl.ds(j * MOE_STRIDE, MOE_ROWS), :] for j in range(ROW_VREGS)], axis=1).astype(_BF16)
        gather(blk + 1)
        scatter(blk - 1)
        y = None
        for hs in (slice(0, D_EXPERT // 2), slice(D_EXPERT // 2, D_EXPERT)):
            g = _dot(xb_scr[...], wg_ref[ee, :, hs])
            u = _dot(xb_scr[...], wu_ref[ee, :, hs])
            act = (g * _sigmoid(g) * u).astype(_BF16)
            yh = _dot(act, wdn_ref[ee, hs, :])
            y = yh if y is None else y + yh
        for j in range(ROW_VREGS):
            yc_scr[pl.ds(j * MOE_STRIDE, MOE_ROWS), :] = y[:, j * LANES:(j + 1) * LANES]

    for ee in range(MOE_EXPERTS_PER_STEP):
        idx = s * N_EXPERTS + jnp.minimum(e, n_esteps - 1) * MOE_EXPERTS_PER_STEP + ee
        nb = jnp.where(e < n_esteps, nblk_ref[idx], 0)
        b0 = bstart_ref[idx]

        def body(b, carry, ee=ee, b0=b0):
            block(ee, b0 + b)
            return carry
        lax.fori_loop(0, nb, body, 0)

    @pl.when(e == n_esteps - 1)
    def _():
        scatter(b0 + nb - 1)

    @pl.when(e >= n_esteps)
    def _():
        row0 = (e - n_esteps) * tlf
        rcf = min(tlf, 128)

        def body(i, carry):
            r = pl.multiple_of(i * rcf, rcf)
            f = jnp.concatenate(
                [acc[pl.ds((row0 + r) * ROW_VREGS + j, rcf, stride=ROW_VREGS), :] for j in range(ROW_VREGS)],
                axis=1)
            x2 = x1_ref[pl.ds(r, rcf), :] + mod_ref[...] * f
            ms = jnp.mean(x2 * x2, axis=-1, keepdims=True)
            y_ref[pl.ds(r, rcf), :] = x2 * lax.rsqrt(ms + EPS) * g_ref[...]
            return carry
        lax.fori_loop(0, tlf // rcf, body, 0)


def _moe(nblk, bstart, tok, wt, n2t, w_gu, w_dn, x1, mod3, boff, g_final, s_tok, tlf, p_max):
    bsz, seq, _ = x1.shape
    n_tok = bsz * seq
    n_super = n_tok // s_tok
    n_esteps = N_EXPERTS // MOE_EXPERTS_PER_STEP
    per = s_tok // tlf
    kern = functools.partial(_moe_kernel, s_tok=s_tok, tlf=tlf, unroll=8)

    def estep(e):
        return jnp.minimum(e, n_esteps - 1)

    def ostep(s, e):
        return s * per + jnp.maximum(e - n_esteps, 0)

    grid_spec = pltpu.PrefetchScalarGridSpec(
        num_scalar_prefetch=2,
        grid=(n_super, n_esteps + per),
        in_specs=[
            pl.BlockSpec((p_max,), lambda s, e, *_: (s,), memory_space=pltpu.SMEM),
            pl.BlockSpec((p_max,), lambda s, e, *_: (s,), memory_space=pltpu.SMEM),
            pl.BlockSpec((s_tok, ROW_VREGS, LANES), lambda s, e, *_: (s, 0, 0),
                         pipeline_mode=pl.Buffered(1)),
            pl.BlockSpec((MOE_EXPERTS_PER_STEP, D, D_EXPERT), lambda s, e, *_: (estep(e), 0, 0)),
            pl.BlockSpec((MOE_EXPERTS_PER_STEP, D, D_EXPERT), lambda s, e, *_: (estep(e), 0, 0)),
            pl.BlockSpec((MOE_EXPERTS_PER_STEP, D_EXPERT, D), lambda s, e, *_: (estep(e), 0, 0)),
            pl.BlockSpec((tlf, D), lambda s, e, *_: (ostep(s, e), 0)),
            pl.BlockSpec((None, 1, D), lambda s, e, *_: (ostep(s, e) * tlf // seq + boff, 0, N_ADA - 1)),
            pl.BlockSpec((1, D), lambda s, e, *_: (0, 0)),
        ],
        out_specs=pl.BlockSpec((tlf, D), lambda s, e, *_: (ostep(s, e), 0)),
        scratch_shapes=[
            pltpu.VMEM(((s_tok + MOE_DUMP) * ROW_VREGS, LANES), _F32),
            pltpu.VMEM((ROW_VREGS * MOE_STRIDE, LANES), _F32),
            pltpu.VMEM((ROW_VREGS * MOE_STRIDE, LANES), _F32),
            pltpu.VMEM((MOE_ROWS, D), _BF16),
        ],
    )
    y = pl.pallas_call(
        kern,
        grid_spec=grid_spec,
        out_shape=jax.ShapeDtypeStruct((n_tok, D), _F32),
        compiler_params=pltpu.CompilerParams(
            dimension_semantics=("arbitrary", "arbitrary"), vmem_limit_bytes=VMEM_LIMIT),
        name="moe",
    )(nblk, bstart, tok, wt, n2t, w_gu[0], w_gu[1], w_dn, x1.reshape(n_tok, D), mod3, g_final)
    return y.reshape(bsz, seq, D)


def _dispatch(route, cnt, s_tok):
    bsz, _, seq = route.shape
    n_tok = bsz * seq
    n_super = n_tok // s_tok
    r = jnp.transpose(route[:, 0:4, :], (1, 0, 2)).reshape(4, n_super, s_tok)
    e_real = jnp.concatenate([r[0], r[1]], axis=1).astype(jnp.int32)
    w_real = jnp.concatenate([r[2], r[3]], axis=1)
    t_real = jnp.broadcast_to(jnp.tile(jnp.arange(s_tok, dtype=jnp.int32), 2)[None, :], e_real.shape)
    experts = jnp.arange(N_EXPERTS, dtype=jnp.int32)
    counts = jnp.sum(cnt.reshape(n_super, -1, N_EXPERTS, LANES), axis=(1, 3)).astype(jnp.int32)
    n_pad = (-counts) % MOE_ROWS
    cand = jnp.arange(MOE_ROWS - 1, dtype=jnp.int32)
    key_pad = jnp.where(cand[None, None, :] < n_pad[:, :, None], experts[None, :, None], N_EXPERTS)
    key_pad = key_pad.reshape(n_super, N_EXPERTS * (MOE_ROWS - 1)).astype(jnp.int32)
    n_sort = 2 * s_tok + N_EXPERTS * (MOE_ROWS - 1)
    p_alloc = (n_sort + MOE_ROWS + 1023) // 1024 * 1024
    tok_bits = s_tok.bit_length()
    keys = jnp.concatenate([(e_real << tok_bits) | t_real, (key_pad << tok_bits) | s_tok], axis=1)
    wts = jnp.concatenate([w_real, jnp.zeros((n_super, n_sort - 2 * s_tok), _F32)], axis=1)
    keys, wt = lax.sort((keys, wts), dimension=1, num_keys=1)
    tok = keys & ((1 << tok_bits) - 1)
    tok = jnp.pad(tok, ((0, 0), (MOE_LEAD, p_alloc - n_sort)), constant_values=s_tok)
    wt = jnp.pad(wt, ((0, 0), (MOE_LEAD, p_alloc - n_sort)))
    p_alloc += MOE_LEAD
    pcounts = counts + n_pad
    pstarts = jnp.cumsum(pcounts, axis=1) - pcounts
    nblk = (pcounts // MOE_ROWS).reshape(-1)
    bstart = (pstarts // MOE_ROWS).reshape(-1)
    return nblk, bstart, tok.reshape(-1), wt.reshape(-1), p_alloc


def _stream(x, mod3, boff, hc, hp, start_pos, mix_w, w_gu, w_dn, g_final, tl, rc, s_tok, tlf):
    bsz, seq, _ = x.shape
    x1, n2t, route, cnt, ncs, nps = _mixer(x, mod3, boff, hc, hp, start_pos, mix_w, tl, rc)
    nblk, bstart, tok, wt, p_alloc = _dispatch(route, cnt, s_tok)
    n2t3 = n2t.reshape(bsz * seq, ROW_VREGS, LANES)
    y = _moe(nblk, bstart, tok, wt, n2t3, w_gu, w_dn, x1, mod3, boff, g_final, s_tok, tlf, p_alloc)
    return y, ncs[None, :, CONV_PAD - (CONV_WIDTH - 1):, :], nps[None, :, 1:, :]


def kernel(x_prompt, x_sample, state_conv, state_pool, c_prompt, c_sample, w_ada, b_ada, g_mix, w_in, w_dw, b_dw, g_ln, b_ln, w_pw, w_pool, s_pool, w_out, g_ffn, w_rg, b_rg, w_re, b_re, w_gate, w_up, w_down, g_final):
    bp, lp, _ = x_prompt.shape
    bs, ls, _ = x_sample.shape
    past_len = 1024

    c_all = jnp.concatenate([c_prompt, c_sample], axis=0)
    mod = _ada(c_all, w_ada[0], b_ada[0][None, :])
    mod3 = mod[:, None, :]

    w_r = jnp.concatenate(
        [w_rg[0], w_re[0], jnp.zeros((D, N_ROUTE - N_GROUPS - N_EXPERTS), _F32)], axis=1).T.astype(_BF16)
    b_r = jnp.concatenate(
        [b_rg[0], b_re[0], jnp.zeros((N_ROUTE - N_GROUPS - N_EXPERTS,), _F32)])[:, None]
    w_dw_p = jnp.concatenate([w_dw[0], jnp.zeros((1, D), _F32)], axis=0)
    mix_w = (g_mix[0][None, :], w_in[0].astype(_BF16), w_dw_p, b_dw[0][None, :], g_ln[0][None, :],
             b_ln[0][None, :], w_pw[0].astype(_BF16), w_pool[0].astype(_BF16), s_pool[0][None, :],
             w_out[0].astype(_BF16), g_ffn[0][None, :], w_r, b_r)
    w_gu = (w_gate[0].astype(_BF16), w_up[0].astype(_BF16))
    w_dn = w_down[0].astype(_BF16)
    g_fin = g_final[None, :]

    hc_p = jnp.zeros((bp, CONV_PAD, D), _F32)
    hp_p = jnp.zeros((bp, POOL_PAD, D), _F32)
    hc_s = jnp.pad(state_conv[0], ((0, 0), (CONV_PAD - (CONV_WIDTH - 1), 0), (0, 0)))
    hp_s = jnp.pad(state_pool[0], ((0, 0), (1, 0), (0, 0)))

    y_p, ncp, npp = _stream(x_prompt, mod3, 0, hc_p, hp_p, 0, mix_w, w_gu, w_dn, g_fin,
                            tl=512, rc=128, s_tok=4096, tlf=512)
    y_s, ncs, nps = _stream(x_sample, mod3, bp, hc_s, hp_s, past_len, mix_w, w_gu, w_dn, g_fin,
                            tl=ls, rc=ls, s_tok=bs * ls, tlf=ls)
    return (y_p, y_s, ncp, npp, ncs, nps)
```

```python
import functools

import jax
import jax.numpy as jnp
from jax import lax
from jax.experimental import pallas as pl
from jax.experimental.pallas import tpu as pltpu

D = 1024
N_IN = 5
N_ADA = 6
CONV_WIDTH = 31
CONV_PAD = 32
POOL_PAD = 16
HIST_STRIDE = 2
CONV_ROWS = 64
POOL_WINDOWS = (2, 4, 8, 16)
POOL_GROUP_W = D // 4
N_GROUPS = 4
EPG = 8
N_EXPERTS = N_GROUPS * EPG
N_ROUTE = 40
D_EXPERT = D // 2
EPS = 1e-6
LANES = 128
SUBLANES = 8
ROW_VREGS = D // LANES
MOE_ROWS = 144
MOE_STRIDE = MOE_ROWS + 8
MOE_LEAD = 1024
MOE_DUMP = 8
MOE_EXPERTS_PER_STEP = 2
VMEM_LIMIT = 56 * 1024 * 1024

_F32 = jnp.float32
_BF16 = jnp.bfloat16


def _dot(a, b):
    return jnp.dot(a, b, preferred_element_type=_F32)


def _sigmoid(x):
    return jax.nn.sigmoid(x)


def _sigmoid_eup(x):
    return 0.5 * jnp.tanh(0.5 * x) + 0.5


def _round_bf16(x):
    return x.astype(_BF16).astype(_F32)


def _ada_kernel(c_ref, w_ref, b_ref, o_ref):
    c = c_ref[...]
    s = (c * _sigmoid(c)).astype(_BF16)
    o_ref[...] = _dot(s, w_ref[...].astype(_BF16)) + b_ref[...]


def _ada(c_all, w_ada, b_ada):
    nb = c_all.shape[0]
    return pl.pallas_call(
        _ada_kernel,
        grid=(N_ADA,),
        in_specs=[
            pl.BlockSpec((nb, D), lambda j: (0, 0)),
            pl.BlockSpec((D, D), lambda j: (0, j)),
            pl.BlockSpec((1, D), lambda j: (0, j)),
        ],
        out_specs=pl.BlockSpec((nb, D), lambda j: (0, j)),
        out_shape=jax.ShapeDtypeStruct((nb, N_ADA * D), _F32),
        name="ada",
    )(c_all, w_ada, b_ada)


def _mix_kernel(x_ref, mod_ref, hc_ref, hp_ref, vec_ref, win_ref, w3_ref, wdw_ref,
                wpw_ref, wpool_ref, wout_ref, wr_ref, br_ref,
                x1_ref, n2t_ref, route_ref, cnt_ref, ncs_ref, nps_ref,
                nb_scr, vbuf, ubuf, t1, t2, z3, ab_scr, pb_scr, *, tl, rc, start_pos):
    l = pl.program_id(1)
    nchunk = tl // rc
    slab = 3 * D // nchunk
    gmix_ref, bdw_ref, gln_ref, bln_ref, spool_ref, gffn_ref = (vec_ref.at[pl.ds(i, 1), :] for i in range(6))

    def chunks(body):
        def step(i, carry):
            body(pl.multiple_of(i * rc, rc))
            return carry
        lax.fori_loop(0, nchunk, step, 0)

    def trow(j, n):
        return pl.ds(HIST_STRIDE * j, n, stride=HIST_STRIDE)

    def hist_put(buf, j, val):
        for lc in range(ROW_VREGS):
            buf[lc, trow(j, val.shape[0]), :] = val[:, lc * LANES:(lc + 1) * LANES]

    def hist_get(buf, j, n):
        return jnp.concatenate([buf[lc, trow(j, n), :] for lc in range(ROW_VREGS)], axis=1)

    @pl.when(l == 0)
    def _():
        hist_put(vbuf, 0, _round_bf16(hc_ref[...]))
        hist_put(ubuf, 0, hp_ref[...])

    sh1 = mod_ref[:, 0 * D:1 * D]
    sc1 = mod_ref[:, 1 * D:2 * D]
    gt1 = mod_ref[:, 2 * D:3 * D]
    sh2 = mod_ref[:, 3 * D:4 * D]
    sc2 = mod_ref[:, 4 * D:5 * D]
    gain1 = gmix_ref[...] * (1.0 + sc1)
    gain2 = gffn_ref[...] * (1.0 + sc2)

    def norm1(r):
        xc = x_ref[pl.ds(r, rc), :]
        ms = jnp.mean(xc * xc, axis=-1, keepdims=True)
        nb_scr[pl.ds(r, rc), :] = (xc * lax.rsqrt(ms + EPS) * gain1 + sh1).astype(_BF16)
    chunks(norm1)

    t1[...] = _dot(nb_scr[...], win_ref[:, 0 * D:1 * D])
    t2[...] = _dot(nb_scr[...], win_ref[:, 1 * D:2 * D])

    def glu(r):
        v = t1[pl.ds(r, rc), :] * _sigmoid_eup(t2[pl.ds(r, rc), :])
        hist_put(vbuf, CONV_PAD + r, _round_bf16(v))
    chunks(glu)
    ncs_ref[...] = t1[tl - CONV_PAD:tl, :] * _sigmoid_eup(t2[tl - CONV_PAD:tl, :])

    def z3_cols(rows, col0, width):
        si, off = divmod(col0, slab)
        assert off + width <= slab
        return z3[si, rows, off:off + width]

    def conv(c, carry):
        r = pl.multiple_of(c * rc, rc)
        for lc in range(ROW_VREGS):
            cs = slice(lc * LANES, (lc + 1) * LANES)
            taps = [jnp.broadcast_to(wdw_ref[k:k + 1, cs], (SUBLANES, LANES)) for k in range(CONV_WIDTH)]
            bias = jnp.broadcast_to(bdw_ref[:, cs], (SUBLANES, LANES))
            ngrp = min(rc, CONV_ROWS) // SUBLANES
            for r2 in range(0, rc, ngrp * SUBLANES):
                accs = [bias] * ngrp
                for o in range((ngrp - 1) * SUBLANES + CONV_WIDTH):
                    xw = vbuf[lc, trow(r + r2 + o + (CONV_PAD - CONV_WIDTH + 1), SUBLANES), :]
                    for i in range(ngrp):
                        k = o - i * SUBLANES
                        if 0 <= k < CONV_WIDTH:
                            accs[i] = accs[i] + taps[k] * xw
                for i in range(ngrp):
                    t1[pl.ds(r + r2 + i * SUBLANES, SUBLANES), cs] = accs[i]
        cv = t1[pl.ds(r, rc), :]
        mu = jnp.mean(cv, axis=-1, keepdims=True)
        xc = cv - mu
        a = xc * lax.rsqrt(jnp.mean(xc * xc, axis=-1, keepdims=True) + EPS) * gln_ref[...] + bln_ref[...]
        ab_scr[pl.ds(r, rc), :] = (a * _sigmoid(a)).astype(_BF16)
        z3[c] = _dot(nb_scr[...], w3_ref[c])
        return carry
    lax.fori_loop(0, nchunk, conv, 0)

    hist_put(ubuf, POOL_PAD, jnp.concatenate(
        [z3_cols(slice(None), q * LANES, LANES) for q in range(ROW_VREGS)], axis=1))
    t1[...] = _dot(ab_scr[...], wpw_ref[...])

    for c in range(nchunk):
        r = c * rc
        pos = start_pos + l * tl + r + lax.broadcasted_iota(jnp.int32, (rc, 1), 0)
        for lc in range(ROW_VREGS):
            cs = slice(lc * LANES, (lc + 1) * LANES)
            w = POOL_WINDOWS[lc * LANES // POOL_GROUP_W]
            u = z3_cols(slice(r, r + rc), lc * LANES, LANES)
            sw = u
            for i in range(1, w):
                sw = sw + ubuf[lc, trow(POOL_PAD + r - i, rc), :]
            cnt = jnp.minimum(pos + 1, w).astype(_F32)
            pb_scr[r:r + rc, cs] = (sw * (1.0 / cnt) - u).astype(_BF16)

    for g in range(N_GROUPS):
        gs = slice(g * POOL_GROUP_W, (g + 1) * POOL_GROUP_W)
        t2[:, gs] = _dot(pb_scr[:, gs], wpool_ref[g]) * spool_ref[:, gs]

    def mix(r):
        rs = pl.ds(r, rc)
        for q in range(D // POOL_GROUP_W):
            qs = slice(q * POOL_GROUP_W, (q + 1) * POOL_GROUP_W)
            ga = z3_cols(rs, D + q * POOL_GROUP_W, POOL_GROUP_W)
            gb = z3_cols(rs, 2 * D + q * POOL_GROUP_W, POOL_GROUP_W)
            ab_scr[rs, qs] = (_sigmoid_eup(ga) * t1[rs, qs] + _sigmoid_eup(gb) * t2[rs, qs]).astype(_BF16)
    chunks(mix)

    t1[...] = _dot(ab_scr[...], wout_ref[...])

    def resid(r):
        rs = pl.ds(r, rc)
        x1 = x_ref[rs, :] + gt1 * t1[rs, :]
        x1_ref[rs, :] = x1
        ms = jnp.mean(x1 * x1, axis=-1, keepdims=True)
        n2 = x1 * lax.rsqrt(ms + EPS) * gain2 + sh2
        nb_scr[rs, :] = n2.astype(_BF16)
        for j in range(ROW_VREGS):
            n2t_ref[pl.ds(r * ROW_VREGS + j, rc, stride=ROW_VREGS), :] = n2[:, j * LANES:(j + 1) * LANES]
    chunks(resid)

    lg = lax.dot_general(wr_ref[...], nb_scr[...], (((1,), (1,)), ((), ())),
                         preferred_element_type=_F32) + br_ref[...]
    lgg = lg[0:N_GROUPS, :]
    mg = jnp.max(lgg, axis=0, keepdims=True)
    p_g = 1.0 / jnp.sum(jnp.exp(lgg - mg), axis=0, keepdims=True)
    gi = lax.broadcasted_iota(jnp.int32, lgg.shape, 0)
    g_idx = jnp.min(jnp.where(lgg == mg, gi, N_GROUPS), axis=0, keepdims=True)
    sel = jnp.zeros((EPG, tl), _F32)
    for g in range(N_GROUPS):
        sel = jnp.where(g_idx == g, lg[N_GROUPS + g * EPG:N_GROUPS + (g + 1) * EPG, :], sel)
    ms = jnp.max(sel, axis=0, keepdims=True)
    es = jnp.exp(sel - ms)
    pe = es / jnp.sum(es, axis=0, keepdims=True)
    ei = lax.broadcasted_iota(jnp.int32, pe.shape, 0)
    v1 = jnp.max(pe, axis=0, keepdims=True)
    i1 = jnp.min(jnp.where(pe == v1, ei, EPG), axis=0, keepdims=True)
    pe2 = jnp.where(ei == i1, -1.0, pe)
    v2 = jnp.max(pe2, axis=0, keepdims=True)
    i2 = jnp.min(jnp.where(pe2 == v2, ei, EPG), axis=0, keepdims=True)
    scale = p_g / (v1 + v2)
    route_ref[0:1, :] = (g_idx * EPG + i1).astype(_F32)
    route_ref[1:2, :] = (g_idx * EPG + i2).astype(_F32)
    route_ref[2:3, :] = v1 * scale
    route_ref[3:4, :] = v2 * scale
    route_ref[4:8, :] = jnp.zeros((4, tl), _F32)

    xi = lax.broadcasted_iota(jnp.int32, (N_EXPERTS, tl), 0)
    hit = ((xi == g_idx * EPG + i1).astype(_F32) + (xi == g_idx * EPG + i2).astype(_F32))
    if tl % LANES == 0:
        cnt_ref[...] = sum(hit[:, q * LANES:(q + 1) * LANES] for q in range(tl // LANES))
    else:
        cnt_ref[...] = jnp.zeros((N_EXPERTS, LANES), _F32)
        cnt_ref[:, 0:tl] = hit

    utail = hist_get(ubuf, tl, POOL_PAD)
    nps_ref[...] = utail
    hist_put(vbuf, 0, hist_get(vbuf, tl, CONV_PAD))
    hist_put(ubuf, 0, utail)


def _const_spec(shape):
    nd = len(shape)
    return pl.BlockSpec(shape, lambda b, l: (0,) * nd, pipeline_mode=pl.Buffered(1))


def _mixer(x, mod3, boff, hc, hp, start_pos, wts, tl, rc):
    bsz, seq, _ = x.shape
    nl = seq // tl
    nchunk = tl // rc
    slab = 3 * D // nchunk
    (g_mix, w_in, w_dw, b_dw, g_ln, b_ln, w_pw, w_pool, s_pool, w_out, g_ffn, w_r, b_r) = wts
    vecs = jnp.concatenate([g_mix, b_dw, g_ln, b_ln, s_pool, g_ffn, jnp.zeros((2, D), _F32)], axis=0)
    w3 = jnp.transpose(w_in[:, 2 * D:].reshape(D, nchunk, slab), (1, 0, 2))
    wts = (vecs, w_in[:, :2 * D], w3, w_dw, w_pw, w_pool, w_out, w_r, b_r)
    kern = functools.partial(_mix_kernel, tl=tl, rc=rc, start_pos=start_pos)
    in_specs = [
        pl.BlockSpec((None, tl, D), lambda b, l: (b, l, 0)),
        pl.BlockSpec((None, 1, N_ADA * D), lambda b, l: (b + boff, 0, 0)),
        pl.BlockSpec((None, CONV_PAD, D), lambda b, l: (b, 0, 0)),
        pl.BlockSpec((None, POOL_PAD, D), lambda b, l: (b, 0, 0)),
    ] + [_const_spec(w.shape) for w in wts]
    out_specs = [
        pl.BlockSpec((None, tl, D), lambda b, l: (b, l, 0)),
        pl.BlockSpec((tl * ROW_VREGS, LANES), lambda b, l: (b * nl + l, 0)),
        pl.BlockSpec((None, SUBLANES, tl), lambda b, l: (b, 0, l)),
        pl.BlockSpec((None, N_EXPERTS, LANES), lambda b, l: (b * nl + l, 0, 0)),
        pl.BlockSpec((None, CONV_PAD, D), lambda b, l: (b, 0, 0)),
        pl.BlockSpec((None, POOL_PAD, D), lambda b, l: (b, 0, 0)),
    ]
    out_shape = [
        jax.ShapeDtypeStruct((bsz, seq, D), _F32),
        jax.ShapeDtypeStruct((bsz * seq * ROW_VREGS, LANES), _F32),
        jax.ShapeDtypeStruct((bsz, SUBLANES, seq), _F32),
        jax.ShapeDtypeStruct((bsz * nl, N_EXPERTS, LANES), _F32),
        jax.ShapeDtypeStruct((bsz, CONV_PAD, D), _F32),
        jax.ShapeDtypeStruct((bsz, POOL_PAD, D), _F32),
    ]
    scratch = [
        pltpu.VMEM((tl, D), _BF16),
        pltpu.VMEM((ROW_VREGS, HIST_STRIDE * (tl + CONV_PAD), LANES), _F32),
        pltpu.VMEM((ROW_VREGS, HIST_STRIDE * (tl + POOL_PAD), LANES), _F32),
        pltpu.VMEM((tl, D), _F32),
        pltpu.VMEM((tl, D), _F32),
        pltpu.VMEM((nchunk, tl, slab), _F32),
        pltpu.VMEM((tl, D), _BF16),
        pltpu.VMEM((tl, D), _BF16),
    ]
    return pl.pallas_call(
        kern,
        grid=(bsz, nl),
        in_specs=in_specs,
        out_specs=out_specs,
        out_shape=out_shape,
        scratch_shapes=scratch,
        compiler_params=pltpu.CompilerParams(
            dimension_semantics=("arbitrary", "arbitrary"), vmem_limit_bytes=VMEM_LIMIT),
        name="mixer",
    )(x, mod3, hc, hp, *wts)


def _moe_kernel(nblk_ref, bstart_ref, tok_ref, src_ref, wgu_ref, wdn_ref, x1_ref, mg_ref,
                y_ref, acc, xc_scr, yc_scr, xb_scr, *, s_tok, tlf, p_max, unroll):
    s = pl.program_id(0)
    e = pl.program_id(1)
    n_esteps = N_EXPERTS // MOE_EXPERTS_PER_STEP

    def arow(t):
        return pl.ds(pl.multiple_of(t * ROW_VREGS, ROW_VREGS), ROW_VREGS)

    def gather(blk):
        base = MOE_LEAD + blk * MOE_ROWS
        for mi in range(MOE_ROWS):
            t = jnp.minimum(tok_ref[base + mi], s_tok - 1)
            xc_scr[pl.ds(mi, ROW_VREGS, stride=MOE_STRIDE), :] = src_ref[t]

    def scatter(blk):
        base = MOE_LEAD + blk * MOE_ROWS
        for g0 in range(0, MOE_ROWS, unroll):
            ts = [tok_ref[base + g0 + i] for i in range(unroll)]
            vals = [acc[arow(ts[i]), :] + lax.bitcast_convert_type(tok_ref[p_max + base + g0 + i], _F32)
                    * yc_scr[pl.ds(g0 + i, ROW_VREGS, stride=MOE_STRIDE), :] for i in range(unroll)]
            for i in range(unroll):
                acc[arow(ts[i]), :] = vals[i]

    @pl.when(e == 0)
    def _():
        acc[...] = jnp.zeros(acc.shape, _F32)
        yc_scr[...] = jnp.zeros(yc_scr.shape, _F32)
        gather(0)

    def block(ee, blk):
        xb_scr[...] = jnp.concatenate(
            [xc_scr[pl.ds(j * MOE_STRIDE, MOE_ROWS), :] for j in range(ROW_VREGS)], axis=1).astype(_BF16)
        gather(blk + 1)
        scatter(blk - 1)
        y = None
        for hs in (slice(0, D_EXPERT // 2), slice(D_EXPERT // 2, D_EXPERT)):
            g = _dot(xb_scr[...], wgu_ref[ee, :, hs])
            u = _dot(xb_scr[...], wgu_ref[ee, :, D_EXPERT + hs.start:D_EXPERT + hs.stop])
            act = (g * _sigmoid(g) * u).astype(_BF16)
            yh = _dot(act, wdn_ref[ee, hs, :])
            y = yh if y is None else y + yh
        for j in range(ROW_VREGS):
            yc_scr[pl.ds(j * MOE_STRIDE, MOE_ROWS), :] = y[:, j * LANES:(j + 1) * LANES]

    for ee in range(MOE_EXPERTS_PER_STEP):
        idx = s * N_EXPERTS + jnp.minimum(e, n_esteps - 1) * MOE_EXPERTS_PER_STEP + ee
        nb = jnp.where(e < n_esteps, nblk_ref[idx], 0)
        b0 = bstart_ref[idx]

        def body(b, carry, ee=ee, b0=b0):
            block(ee, b0 + b)
            return carry
        lax.fori_loop(0, nb, body, 0)

    @pl.when(e == n_esteps - 1)
    def _():
        scatter(b0 + nb - 1)

    @pl.when(e >= n_esteps)
    def _():
        row0 = (e - n_esteps) * tlf
        rcf = min(tlf, 128)

        def body(i, carry):
            r = pl.multiple_of(i * rcf, rcf)
            f = jnp.concatenate(
                [acc[pl.ds((row0 + r) * ROW_VREGS + j, rcf, stride=ROW_VREGS), :] for j in range(ROW_VREGS)],
                axis=1)
            x2 = x1_ref[pl.ds(r, rcf), :] + mg_ref[0:1, :] * f
            ms = jnp.mean(x2 * x2, axis=-1, keepdims=True)
            y_ref[pl.ds(r, rcf), :] = x2 * lax.rsqrt(ms + EPS) * mg_ref[1:2, :]
            return carry
        lax.fori_loop(0, tlf // rcf, body, 0)


def _moe(nblk, bstart, tok, wt, n2t, w_gu, w_dn, x1, mod3, boff, g_final, s_tok, tlf, p_max):
    bsz, seq, _ = x1.shape
    n_tok = bsz * seq
    n_super = n_tok // s_tok
    n_esteps = N_EXPERTS // MOE_EXPERTS_PER_STEP
    per = s_tok // tlf
    kern = functools.partial(_moe_kernel, s_tok=s_tok, tlf=tlf, p_max=p_max, unroll=8)
    idx = jnp.concatenate([tok.reshape(n_super, p_max), lax.bitcast_convert_type(wt, jnp.int32).reshape(n_super, p_max)],
                          axis=1).reshape(-1)
    mg = jnp.concatenate([mod3[:, :, (N_ADA - 1) * D:], jnp.broadcast_to(g_final[None], (mod3.shape[0], 1, D))], axis=1)

    def estep(e):
        return jnp.minimum(e, n_esteps - 1)

    def ostep(s, e):
        return s * per + jnp.maximum(e - n_esteps, 0)

    grid_spec = pltpu.PrefetchScalarGridSpec(
        num_scalar_prefetch=2,
        grid=(n_super, n_esteps + per),
        in_specs=[
            pl.BlockSpec((2 * p_max,), lambda s, e, *_: (s,), memory_space=pltpu.SMEM),
            pl.BlockSpec((s_tok, ROW_VREGS, LANES), lambda s, e, *_: (s, 0, 0),
                         pipeline_mode=pl.Buffered(1)),
            pl.BlockSpec((MOE_EXPERTS_PER_STEP, D, D), lambda s, e, *_: (estep(e), 0, 0)),
            pl.BlockSpec((MOE_EXPERTS_PER_STEP, D_EXPERT, D), lambda s, e, *_: (estep(e), 0, 0)),
            pl.BlockSpec((tlf, D), lambda s, e, *_: (ostep(s, e), 0)),
            pl.BlockSpec((None, 2, D), lambda s, e, *_: (ostep(s, e) * tlf // seq + boff, 0, 0)),
        ],
        out_specs=pl.BlockSpec((tlf, D), lambda s, e, *_: (ostep(s, e), 0)),
        scratch_shapes=[
            pltpu.VMEM(((s_tok + MOE_DUMP) * ROW_VREGS, LANES), _F32),
            pltpu.VMEM((ROW_VREGS * MOE_STRIDE, LANES), _F32),
            pltpu.VMEM((ROW_VREGS * MOE_STRIDE, LANES), _F32),
            pltpu.VMEM((MOE_ROWS, D), _BF16),
        ],
    )
    y = pl.pallas_call(
        kern,
        grid_spec=grid_spec,
        out_shape=jax.ShapeDtypeStruct((n_tok, D), _F32),
        compiler_params=pltpu.CompilerParams(
            dimension_semantics=("arbitrary", "arbitrary"), vmem_limit_bytes=VMEM_LIMIT),
        name="moe",
    )(nblk, bstart, idx, n2t, w_gu, w_dn, x1.reshape(n_tok, D), mg)
    return y.reshape(bsz, seq, D)


def _dispatch(route, cnt, s_tok):
    bsz, _, seq = route.shape
    n_tok = bsz * seq
    n_super = n_tok // s_tok
    r = jnp.transpose(route[:, 0:4, :], (1, 0, 2)).reshape(4, n_super, s_tok)
    e_real = jnp.concatenate([r[0], r[1]], axis=1).astype(jnp.int32)
    w_real = jnp.concatenate([r[2], r[3]], axis=1)
    t_real = jnp.broadcast_to(jnp.tile(jnp.arange(s_tok, dtype=jnp.int32), 2)[None, :], e_real.shape)
    experts = jnp.arange(N_EXPERTS, dtype=jnp.int32)
    counts = jnp.sum(cnt.reshape(n_super, -1, N_EXPERTS, LANES), axis=(1, 3)).astype(jnp.int32)
    n_pad = (-counts) % MOE_ROWS
    cand = jnp.arange(MOE_ROWS - 1, dtype=jnp.int32)
    key_pad = jnp.where(cand[None, None, :] < n_pad[:, :, None], experts[None, :, None], N_EXPERTS)
    key_pad = key_pad.reshape(n_super, N_EXPERTS * (MOE_ROWS - 1)).astype(jnp.int32)
    n_sort = 2 * s_tok + N_EXPERTS * (MOE_ROWS - 1)
    p_alloc = (n_sort + MOE_ROWS + 1023) // 1024 * 1024
    tok_bits = s_tok.bit_length()
    keys = jnp.concatenate([(e_real << tok_bits) | t_real, (key_pad << tok_bits) | s_tok], axis=1)
    wts = jnp.concatenate([w_real, jnp.zeros((n_super, n_sort - 2 * s_tok), _F32)], axis=1)
    keys, wt = lax.sort((keys, wts), dimension=1, num_keys=1)
    tok = keys & ((1 << tok_bits) - 1)
    tok = jnp.pad(tok, ((0, 0), (MOE_LEAD, p_alloc - n_sort)), constant_values=s_tok)
    wt = jnp.pad(wt, ((0, 0), (MOE_LEAD, p_alloc - n_sort)))
    p_alloc += MOE_LEAD
    pcounts = counts + n_pad
    pstarts = jnp.cumsum(pcounts, axis=1) - pcounts
    nblk = (pcounts // MOE_ROWS).reshape(-1)
    bstart = (pstarts // MOE_ROWS).reshape(-1)
    return nblk, bstart, tok.reshape(-1), wt.reshape(-1), p_alloc


def _stream(x, mod3, boff, hc, hp, start_pos, mix_w, w_gu, w_dn, g_final, tl, rc, s_tok, tlf):
    bsz, seq, _ = x.shape
    x1, n2t, route, cnt, ncs, nps = _mixer(x, mod3, boff, hc, hp, start_pos, mix_w, tl, rc)
    nblk, bstart, tok, wt, p_alloc = _dispatch(route, cnt, s_tok)
    n2t3 = n2t.reshape(bsz * seq, ROW_VREGS, LANES)
    y = _moe(nblk, bstart, tok, wt, n2t3, w_gu, w_dn, x1, mod3, boff, g_final, s_tok, tlf, p_alloc)
    return y, ncs[None, :, CONV_PAD - (CONV_WIDTH - 1):, :], nps[None, :, 1:, :]


def kernel(x_prompt, x_sample, state_conv, state_pool, c_prompt, c_sample, w_ada, b_ada, g_mix, w_in, w_dw, b_dw, g_ln, b_ln, w_pw, w_pool, s_pool, w_out, g_ffn, w_rg, b_rg, w_re, b_re, w_gate, w_up, w_down, g_final):
    bp, lp, _ = x_prompt.shape
    bs, ls, _ = x_sample.shape
    past_len = 1024

    c_all = jnp.concatenate([c_prompt, c_sample], axis=0)
    mod = _ada(c_all, w_ada[0], b_ada[0][None, :])
    mod3 = mod[:, None, :]

    w_r = jnp.concatenate(
        [w_rg[0], w_re[0], jnp.zeros((D, N_ROUTE - N_GROUPS - N_EXPERTS), _F32)], axis=1).T.astype(_BF16)
    b_r = jnp.concatenate(
        [b_rg[0], b_re[0], jnp.zeros((N_ROUTE - N_GROUPS - N_EXPERTS,), _F32)])[:, None]
    w_dw_p = jnp.concatenate([w_dw[0], jnp.zeros((1, D), _F32)], axis=0)
    mix_w = (g_mix[0][None, :], w_in[0].astype(_BF16), w_dw_p, b_dw[0][None, :], g_ln[0][None, :],
             b_ln[0][None, :], w_pw[0].astype(_BF16), w_pool[0].astype(_BF16), s_pool[0][None, :],
             w_out[0].astype(_BF16), g_ffn[0][None, :], w_r, b_r)
    w_gu = jnp.concatenate([w_gate[0], w_up[0]], axis=2).astype(_BF16)
    w_dn = w_down[0].astype(_BF16)
    g_fin = g_final[None, :]

    hc_p = jnp.zeros((bp, CONV_PAD, D), _F32)
    hp_p = jnp.zeros((bp, POOL_PAD, D), _F32)
    hc_s = jnp.pad(state_conv[0], ((0, 0), (CONV_PAD - (CONV_WIDTH - 1), 0), (0, 0)))
    hp_s = jnp.pad(state_pool[0], ((0, 0), (1, 0), (0, 0)))

    y_p, ncp, npp = _stream(x_prompt, mod3, 0, hc_p, hp_p, 0, mix_w, w_gu, w_dn, g_fin,
                            tl=512, rc=128, s_tok=4096, tlf=512)
    y_s, ncs, nps = _stream(x_sample, mod3, bp, hc_s, hp_s, past_len, mix_w, w_gu, w_dn, g_fin,
                            tl=ls, rc=ls, s_tok=bs * ls, tlf=ls)
    return (y_p, y_s, ncp, npp, ncs, nps)
```

```python
import functools

import jax
import jax.numpy as jnp
from jax import lax
from jax.experimental import pallas as pl
from jax.experimental.pallas import tpu as pltpu

D = 1024
N_IN = 5
N_ADA = 6
CONV_WIDTH = 31
CONV_PAD = 32
POOL_PAD = 16
HIST_STRIDE = 2
CONV_ROWS = 64
POOL_WINDOWS = (2, 4, 8, 16)
POOL_GROUP_W = D // 4
N_GROUPS = 4
EPG = 8
N_EXPERTS = N_GROUPS * EPG
N_ROUTE = 40
D_EXPERT = D // 2
EPS = 1e-6
LANES = 128
SUBLANES = 8
ROW_VREGS = D // LANES
MOE_ROWS = 144
MOE_STRIDE = MOE_ROWS + 8
MOE_LEAD = 1024
MOE_DUMP = 8
MOE_EXPERTS_PER_STEP = 2
VMEM_LIMIT = 56 * 1024 * 1024

_F32 = jnp.float32
_BF16 = jnp.bfloat16


def _dot(a, b):
    return jnp.dot(a, b, preferred_element_type=_F32)


def _sigmoid(x):
    return jax.nn.sigmoid(x)


def _sigmoid_eup(x):
    return 0.5 * jnp.tanh(0.5 * x) + 0.5


def _round_bf16(x):
    return x.astype(_BF16).astype(_F32)


def _ada_kernel(c_ref, w_ref, b_ref, o_ref):
    c = c_ref[...]
    s = (c * _sigmoid(c)).astype(_BF16)
    o_ref[...] = _dot(s, w_ref[...].astype(_BF16)) + b_ref[...]


def _ada(c_all, w_ada, b_ada):
    nb = c_all.shape[0]
    return pl.pallas_call(
        _ada_kernel,
        grid=(N_ADA,),
        in_specs=[
            pl.BlockSpec((nb, D), lambda j: (0, 0)),
            pl.BlockSpec((D, D), lambda j: (0, j)),
            pl.BlockSpec((1, D), lambda j: (0, j)),
        ],
        out_specs=pl.BlockSpec((nb, D), lambda j: (0, j)),
        out_shape=jax.ShapeDtypeStruct((nb, N_ADA * D), _F32),
        name="ada",
    )(c_all, w_ada, b_ada)


def _mix_kernel(x_ref, mod_ref, hc_ref, hp_ref, gmix_ref, win_ref, w3_ref, wdw_ref, bdw_ref, gln_ref,
                bln_ref, wpw_ref, wpool_ref, spool_ref, wout_ref, gffn_ref, wr_ref, br_ref,
                x1_ref, n2t_ref, route_ref, ncs_ref, nps_ref,
                nb_scr, vbuf, ubuf, t1, t2, z3, ab_scr, pb_scr, *, tl, rc, start_pos):
    l = pl.program_id(1)
    nchunk = tl // rc
    slab = 3 * D // nchunk

    def chunks(body):
        def step(i, carry):
            body(pl.multiple_of(i * rc, rc))
            return carry
        lax.fori_loop(0, nchunk, step, 0)

    def trow(j, n):
        return pl.ds(HIST_STRIDE * j, n, stride=HIST_STRIDE)

    def hist_put(buf, j, val):
        for lc in range(ROW_VREGS):
            buf[lc, trow(j, val.shape[0]), :] = val[:, lc * LANES:(lc + 1) * LANES]

    def hist_get(buf, j, n):
        return jnp.concatenate([buf[lc, trow(j, n), :] for lc in range(ROW_VREGS)], axis=1)

    @pl.when(l == 0)
    def _():
        hist_put(vbuf, 0, _round_bf16(hc_ref[...]))
        hist_put(ubuf, 0, hp_ref[...])

    sh1 = mod_ref[:, 0 * D:1 * D]
    sc1 = mod_ref[:, 1 * D:2 * D]
    gt1 = mod_ref[:, 2 * D:3 * D]
    sh2 = mod_ref[:, 3 * D:4 * D]
    sc2 = mod_ref[:, 4 * D:5 * D]

    def norm1(r):
        xc = x_ref[pl.ds(r, rc), :]
        ms = jnp.mean(xc * xc, axis=-1, keepdims=True)
        n = xc * lax.rsqrt(ms + EPS) * gmix_ref[...]
        nb_scr[pl.ds(r, rc), :] = (n * (1.0 + sc1) + sh1).astype(_BF16)
    chunks(norm1)

    t1[...] = _dot(nb_scr[...], win_ref[:, 0 * D:1 * D])
    t2[...] = _dot(nb_scr[...], win_ref[:, 1 * D:2 * D])

    def glu(r):
        v = t1[pl.ds(r, rc), :] * _sigmoid_eup(t2[pl.ds(r, rc), :])
        hist_put(vbuf, CONV_PAD + r, _round_bf16(v))
    chunks(glu)
    ncs_ref[...] = t1[tl - CONV_PAD:tl, :] * _sigmoid_eup(t2[tl - CONV_PAD:tl, :])

    def z3_cols(rows, col0, width):
        si, off = divmod(col0, slab)
        assert off + width <= slab
        return z3[si, rows, off:off + width]

    def conv(c, carry):
        r = pl.multiple_of(c * rc, rc)
        for lc in range(ROW_VREGS):
            cs = slice(lc * LANES, (lc + 1) * LANES)
            taps = [jnp.broadcast_to(wdw_ref[k:k + 1, cs], (SUBLANES, LANES)) for k in range(CONV_WIDTH)]
            bias = jnp.broadcast_to(bdw_ref[:, cs], (SUBLANES, LANES))
            ngrp = min(rc, CONV_ROWS) // SUBLANES
            for r2 in range(0, rc, ngrp * SUBLANES):
                accs = [bias] * ngrp
                for o in range((ngrp - 1) * SUBLANES + CONV_WIDTH):
                    xw = vbuf[lc, trow(r + r2 + o + (CONV_PAD - CONV_WIDTH + 1), SUBLANES), :]
                    for i in range(ngrp):
                        k = o - i * SUBLANES
                        if 0 <= k < CONV_WIDTH:
                            accs[i] = accs[i] + taps[k] * xw
                for i in range(ngrp):
                    t1[pl.ds(r + r2 + i * SUBLANES, SUBLANES), cs] = accs[i]
        cv = t1[pl.ds(r, rc), :]
        mu = jnp.mean(cv, axis=-1, keepdims=True)
        xc = cv - mu
        a = xc * lax.rsqrt(jnp.mean(xc * xc, axis=-1, keepdims=True) + EPS) * gln_ref[...] + bln_ref[...]
        ab_scr[pl.ds(r, rc), :] = (a * _sigmoid(a)).astype(_BF16)
        z3[c] = _dot(nb_scr[...], w3_ref[c])
        return carry
    lax.fori_loop(0, nchunk, conv, 0)

    hist_put(ubuf, POOL_PAD, jnp.concatenate(
        [z3_cols(slice(None), q * LANES, LANES) for q in range(ROW_VREGS)], axis=1))
    t1[...] = _dot(ab_scr[...], wpw_ref[...])

    for c in range(nchunk):
        r = c * rc
        pos = start_pos + l * tl + r + lax.broadcasted_iota(jnp.int32, (rc, 1), 0)
        for lc in range(ROW_VREGS):
            cs = slice(lc * LANES, (lc + 1) * LANES)
            w = POOL_WINDOWS[lc * LANES // POOL_GROUP_W]
            u = z3_cols(slice(r, r + rc), lc * LANES, LANES)
            sw = u
            for i in range(1, w):
                sw = sw + ubuf[lc, trow(POOL_PAD + r - i, rc), :]
            cnt = jnp.minimum(pos + 1, w).astype(_F32)
            pb_scr[r:r + rc, cs] = (sw * (1.0 / cnt) - u).astype(_BF16)

    for g in range(N_GROUPS):
        gs = slice(g * POOL_GROUP_W, (g + 1) * POOL_GROUP_W)
        t2[:, gs] = _dot(pb_scr[:, gs], wpool_ref[g]) * spool_ref[:, gs]

    def mix(r):
        rs = pl.ds(r, rc)
        for q in range(D // POOL_GROUP_W):
            qs = slice(q * POOL_GROUP_W, (q + 1) * POOL_GROUP_W)
            ga = z3_cols(rs, D + q * POOL_GROUP_W, POOL_GROUP_W)
            gb = z3_cols(rs, 2 * D + q * POOL_GROUP_W, POOL_GROUP_W)
            ab_scr[rs, qs] = (_sigmoid_eup(ga) * t1[rs, qs] + _sigmoid_eup(gb) * t2[rs, qs]).astype(_BF16)
    chunks(mix)

    t1[...] = _dot(ab_scr[...], wout_ref[...])

    def resid(r):
        rs = pl.ds(r, rc)
        x1 = x_ref[rs, :] + gt1 * t1[rs, :]
        x1_ref[rs, :] = x1
        ms = jnp.mean(x1 * x1, axis=-1, keepdims=True)
        n2 = x1 * lax.rsqrt(ms + EPS) * gffn_ref[...] * (1.0 + sc2) + sh2
        nb_scr[rs, :] = n2.astype(_BF16)
        for j in range(ROW_VREGS):
            n2t_ref[pl.ds(r * ROW_VREGS + j, rc, stride=ROW_VREGS), :] = n2[:, j * LANES:(j + 1) * LANES]
    chunks(resid)

    lg = lax.dot_general(wr_ref[...], nb_scr[...], (((1,), (1,)), ((), ())),
                         preferred_element_type=_F32) + br_ref[...]
    lgg = lg[0:N_GROUPS, :]
    mg = jnp.max(lgg, axis=0, keepdims=True)
    p_g = 1.0 / jnp.sum(jnp.exp(lgg - mg), axis=0, keepdims=True)
    gi = lax.broadcasted_iota(jnp.int32, lgg.shape, 0)
    g_idx = jnp.min(jnp.where(lgg == mg, gi, N_GROUPS), axis=0, keepdims=True)
    sel = jnp.zeros((EPG, tl), _F32)
    for g in range(N_GROUPS):
        sel = jnp.where(g_idx == g, lg[N_GROUPS + g * EPG:N_GROUPS + (g + 1) * EPG, :], sel)
    ms = jnp.max(sel, axis=0, keepdims=True)
    es = jnp.exp(sel - ms)
    pe = es / jnp.sum(es, axis=0, keepdims=True)
    ei = lax.broadcasted_iota(jnp.int32, pe.shape, 0)
    v1 = jnp.max(pe, axis=0, keepdims=True)
    i1 = jnp.min(jnp.where(pe == v1, ei, EPG), axis=0, keepdims=True)
    pe2 = jnp.where(ei == i1, -1.0, pe)
    v2 = jnp.max(pe2, axis=0, keepdims=True)
    i2 = jnp.min(jnp.where(pe2 == v2, ei, EPG), axis=0, keepdims=True)
    scale = p_g / (v1 + v2)
    route_ref[0:1, :] = (g_idx * EPG + i1).astype(_F32)
    route_ref[1:2, :] = (g_idx * EPG + i2).astype(_F32)
    route_ref[2:3, :] = v1 * scale
    route_ref[3:4, :] = v2 * scale
    route_ref[4:8, :] = jnp.zeros((4, tl), _F32)

    utail = hist_get(ubuf, tl, POOL_PAD)
    nps_ref[...] = utail
    hist_put(vbuf, 0, hist_get(vbuf, tl, CONV_PAD))
    hist_put(ubuf, 0, utail)


def _const_spec(shape):
    nd = len(shape)
    return pl.BlockSpec(shape, lambda b, l: (0,) * nd, pipeline_mode=pl.Buffered(1))


def _mixer(x, mod3, boff, hc, hp, start_pos, wts, tl, rc):
    bsz, seq, _ = x.shape
    nl = seq // tl
    nchunk = tl // rc
    slab = 3 * D // nchunk
    w_in = wts[1]
    w3 = jnp.transpose(w_in[:, 2 * D:].reshape(D, nchunk, slab), (1, 0, 2))
    wts = (wts[0], w_in[:, :2 * D], w3) + tuple(wts[2:])
    kern = functools.partial(_mix_kernel, tl=tl, rc=rc, start_pos=start_pos)
    in_specs = [
        pl.BlockSpec((None, tl, D), lambda b, l: (b, l, 0)),
        pl.BlockSpec((None, 1, N_ADA * D), lambda b, l: (b + boff, 0, 0)),
        pl.BlockSpec((None, CONV_PAD, D), lambda b, l: (b, 0, 0)),
        pl.BlockSpec((None, POOL_PAD, D), lambda b, l: (b, 0, 0)),
    ] + [_const_spec(w.shape) for w in wts]
    out_specs = [
        pl.BlockSpec((None, tl, D), lambda b, l: (b, l, 0)),
        pl.BlockSpec((tl * ROW_VREGS, LANES), lambda b, l: (b * nl + l, 0)),
        pl.BlockSpec((None, SUBLANES, tl), lambda b, l: (b, 0, l)),
        pl.BlockSpec((None, CONV_PAD, D), lambda b, l: (b, 0, 0)),
        pl.BlockSpec((None, POOL_PAD, D), lambda b, l: (b, 0, 0)),
    ]
    out_shape = [
        jax.ShapeDtypeStruct((bsz, seq, D), _F32),
        jax.ShapeDtypeStruct((bsz * seq * ROW_VREGS, LANES), _F32),
        jax.ShapeDtypeStruct((bsz, SUBLANES, seq), _F32),
        jax.ShapeDtypeStruct((bsz, CONV_PAD, D), _F32),
        jax.ShapeDtypeStruct((bsz, POOL_PAD, D), _F32),
    ]
    scratch = [
        pltpu.VMEM((tl, D), _BF16),
        pltpu.VMEM((ROW_VREGS, HIST_STRIDE * (tl + CONV_PAD), LANES), _F32),
        pltpu.VMEM((ROW_VREGS, HIST_STRIDE * (tl + POOL_PAD), LANES), _F32),
        pltpu.VMEM((tl, D), _F32),
        pltpu.VMEM((tl, D), _F32),
        pltpu.VMEM((nchunk, tl, slab), _F32),
        pltpu.VMEM((tl, D), _BF16),
        pltpu.VMEM((tl, D), _BF16),
    ]
    return pl.pallas_call(
        kern,
        grid=(bsz, nl),
        in_specs=in_specs,
        out_specs=out_specs,
        out_shape=out_shape,
        scratch_shapes=scratch,
        compiler_params=pltpu.CompilerParams(
            dimension_semantics=("arbitrary", "arbitrary"), vmem_limit_bytes=VMEM_LIMIT),
        name="mixer",
    )(x, mod3, hc, hp, *wts)


def _moe_kernel(nblk_ref, bstart_ref, tok_ref, wt_ref, src_ref, wgu_ref, wdn_ref, x1_ref, mod_ref, g_ref,
                y_ref, acc, xc_scr, yc_scr, xb_scr, *, s_tok, tlf, unroll):
    s = pl.program_id(0)
    e = pl.program_id(1)
    n_esteps = N_EXPERTS // MOE_EXPERTS_PER_STEP

    def arow(t):
        return pl.ds(pl.multiple_of(t * ROW_VREGS, ROW_VREGS), ROW_VREGS)

    def gather(blk):
        base = MOE_LEAD + blk * MOE_ROWS
        for mi in range(MOE_ROWS):
            t = jnp.minimum(tok_ref[base + mi], s_tok - 1)
            xc_scr[pl.ds(mi, ROW_VREGS, stride=MOE_STRIDE), :] = src_ref[t]

    def scatter(blk):
        base = MOE_LEAD + blk * MOE_ROWS
        for g0 in range(0, MOE_ROWS, unroll):
            ts = [tok_ref[base + g0 + i] for i in range(unroll)]
            vals = [acc[arow(ts[i]), :] + wt_ref[base + g0 + i]
                    * yc_scr[pl.ds(g0 + i, ROW_VREGS, stride=MOE_STRIDE), :] for i in range(unroll)]
            for i in range(unroll):
                acc[arow(ts[i]), :] = vals[i]

    @pl.when(e == 0)
    def _():
        acc[...] = jnp.zeros(acc.shape, _F32)
        yc_scr[...] = jnp.zeros(yc_scr.shape, _F32)
        gather(0)

    def block(ee, blk):
        xb_scr[...] = jnp.concatenate(
            [xc_scr[pl.ds(j * MOE_STRIDE, MOE_ROWS), :] for j in range(ROW_VREGS)], axis=1).astype(_BF16)
        gather(blk + 1)
        scatter(blk - 1)
        y = None
        for hs in (slice(0, D_EXPERT // 2), slice(D_EXPERT // 2, D_EXPERT)):
            g = _dot(xb_scr[...], wgu_ref[ee, :, hs])
            u = _dot(xb_scr[...], wgu_ref[ee, :, D_EXPERT + hs.start:D_EXPERT + hs.stop])
            act = (g * _sigmoid(g) * u).astype(_BF16)
            yh = _dot(act, wdn_ref[ee, hs, :])
            y = yh if y is None else y + yh
        for j in range(ROW_VREGS):
            yc_scr[pl.ds(j * MOE_STRIDE, MOE_ROWS), :] = y[:, j * LANES:(j + 1) * LANES]

    for ee in range(MOE_EXPERTS_PER_STEP):
        idx = s * N_EXPERTS + jnp.minimum(e, n_esteps - 1) * MOE_EXPERTS_PER_STEP + ee
        nb = jnp.where(e < n_esteps, nblk_ref[idx], 0)
        b0 = bstart_ref[idx]

        def body(b, carry, ee=ee, b0=b0):
            block(ee, b0 + b)
            return carry
        lax.fori_loop(0, nb, body, 0)

    @pl.when(e == n_esteps - 1)
    def _():
        scatter(b0 + nb - 1)

    @pl.when(e >= n_esteps)
    def _():
        row0 = (e - n_esteps) * tlf
        rcf = min(tlf, 128)

        def body(i, carry):
            r = pl.multiple_of(i * rcf, rcf)
            f = jnp.concatenate(
                [acc[pl.ds((row0 + r) * ROW_VREGS + j, rcf, stride=ROW_VREGS), :] for j in range(ROW_VREGS)],
                axis=1)
            x2 = x1_ref[pl.ds(r, rcf), :] + mod_ref[...] * f
            ms = jnp.mean(x2 * x2, axis=-1, keepdims=True)
            y_ref[pl.ds(r, rcf), :] = x2 * lax.rsqrt(ms + EPS) * g_ref[...]
            return carry
        lax.fori_loop(0, tlf // rcf, body, 0)


def _moe(nblk, bstart, tok, wt, n2t, w_gu, w_dn, x1, mod3, boff, g_final, s_tok, tlf, p_max):
    bsz, seq, _ = x1.shape
    n_tok = bsz * seq
    n_super = n_tok // s_tok
    n_esteps = N_EXPERTS // MOE_EXPERTS_PER_STEP
    per = s_tok // tlf
    kern = functools.partial(_moe_kernel, s_tok=s_tok, tlf=tlf, unroll=8)

    def estep(e):
        return jnp.minimum(e, n_esteps - 1)

    def ostep(s, e):
        return s * per + jnp.maximum(e - n_esteps, 0)

    grid_spec = pltpu.PrefetchScalarGridSpec(
        num_scalar_prefetch=2,
        grid=(n_super, n_esteps + per),
        in_specs=[
            pl.BlockSpec((p_max,), lambda s, e, *_: (s,), memory_space=pltpu.SMEM),
            pl.BlockSpec((p_max,), lambda s, e, *_: (s,), memory_space=pltpu.SMEM),
            pl.BlockSpec((s_tok, ROW_VREGS, LANES), lambda s, e, *_: (s, 0, 0),
                         pipeline_mode=pl.Buffered(1)),
            pl.BlockSpec((MOE_EXPERTS_PER_STEP, D, D), lambda s, e, *_: (estep(e), 0, 0)),
            pl.BlockSpec((MOE_EXPERTS_PER_STEP, D_EXPERT, D), lambda s, e, *_: (estep(e), 0, 0)),
            pl.BlockSpec((tlf, D), lambda s, e, *_: (ostep(s, e), 0)),
            pl.BlockSpec((None, 1, D), lambda s, e, *_: (ostep(s, e) * tlf // seq + boff, 0, N_ADA - 1)),
            pl.BlockSpec((1, D), lambda s, e, *_: (0, 0)),
        ],
        out_specs=pl.BlockSpec((tlf, D), lambda s, e, *_: (ostep(s, e), 0)),
        scratch_shapes=[
            pltpu.VMEM(((s_tok + MOE_DUMP) * ROW_VREGS, LANES), _F32),
            pltpu.VMEM((ROW_VREGS * MOE_STRIDE, LANES), _F32),
            pltpu.VMEM((ROW_VREGS * MOE_STRIDE, LANES), _F32),
            pltpu.VMEM((MOE_ROWS, D), _BF16),
        ],
    )
    y = pl.pallas_call(
        kern,
        grid_spec=grid_spec,
        out_shape=jax.ShapeDtypeStruct((n_tok, D), _F32),
        compiler_params=pltpu.CompilerParams(
            dimension_semantics=("arbitrary", "arbitrary"), vmem_limit_bytes=VMEM_LIMIT),
        name="moe",
    )(nblk, bstart, tok, wt, n2t, w_gu, w_dn, x1.reshape(n_tok, D), mod3, g_final)
    return y.reshape(bsz, seq, D)


def _dispatch(route, s_tok):
    bsz, _, seq = route.shape
    n_tok = bsz * seq
    n_super = n_tok // s_tok
    r = jnp.transpose(route[:, 0:4, :], (1, 0, 2)).reshape(4, n_super, s_tok)
    e_real = jnp.concatenate([r[0], r[1]], axis=1).astype(jnp.int32)
    w_real = jnp.concatenate([r[2], r[3]], axis=1)
    t_real = jnp.broadcast_to(jnp.tile(jnp.arange(s_tok, dtype=jnp.int32), 2)[None, :], e_real.shape)
    experts = jnp.arange(N_EXPERTS, dtype=jnp.int32)
    counts = jnp.sum(e_real[:, :, None] == experts, axis=1).astype(jnp.int32)
    n_pad = (-counts) % MOE_ROWS
    cand = jnp.arange(MOE_ROWS - 1, dtype=jnp.int32)
    key_pad = jnp.where(cand[None, None, :] < n_pad[:, :, None], experts[None, :, None], N_EXPERTS)
    key_pad = key_pad.reshape(n_super, N_EXPERTS * (MOE_ROWS - 1)).astype(jnp.int32)
    n_sort = 2 * s_tok + N_EXPERTS * (MOE_ROWS - 1)
    p_alloc = (n_sort + MOE_ROWS + 1023) // 1024 * 1024
    tok_bits = s_tok.bit_length()
    keys = jnp.concatenate([(e_real << tok_bits) | t_real, (key_pad << tok_bits) | s_tok], axis=1)
    wts = jnp.concatenate([w_real, jnp.zeros((n_super, n_sort - 2 * s_tok), _F32)], axis=1)
    keys, wt = lax.sort((keys, wts), dimension=1, num_keys=1)
    tok = keys & ((1 << tok_bits) - 1)
    tok = jnp.pad(tok, ((0, 0), (MOE_LEAD, p_alloc - n_sort)), constant_values=s_tok)
    wt = jnp.pad(wt, ((0, 0), (MOE_LEAD, p_alloc - n_sort)))
    p_alloc += MOE_LEAD
    pcounts = counts + n_pad
    pstarts = jnp.cumsum(pcounts, axis=1) - pcounts
    nblk = (pcounts // MOE_ROWS).reshape(-1)
    bstart = (pstarts // MOE_ROWS).reshape(-1)
    return nblk, bstart, tok.reshape(-1), wt.reshape(-1), p_alloc


def _stream(x, mod3, boff, hc, hp, start_pos, mix_w, w_gu, w_dn, g_final, tl, rc, s_tok, tlf):
    bsz, seq, _ = x.shape
    x1, n2t, route, ncs, nps = _mixer(x, mod3, boff, hc, hp, start_pos, mix_w, tl, rc)
    nblk, bstart, tok, wt, p_alloc = _dispatch(route, s_tok)
    n2t3 = n2t.reshape(bsz * seq, ROW_VREGS, LANES)
    y = _moe(nblk, bstart, tok, wt, n2t3, w_gu, w_dn, x1, mod3, boff, g_final, s_tok, tlf, p_alloc)
    return y, ncs[None, :, CONV_PAD - (CONV_WIDTH - 1):, :], nps[None, :, 1:, :]


def kernel(x_prompt, x_sample, state_conv, state_pool, c_prompt, c_sample, w_ada, b_ada, g_mix, w_in, w_dw, b_dw, g_ln, b_ln, w_pw, w_pool, s_pool, w_out, g_ffn, w_rg, b_rg, w_re, b_re, w_gate, w_up, w_down, g_final):
    bp, lp, _ = x_prompt.shape
    bs, ls, _ = x_sample.shape
    past_len = 1024

    c_all = jnp.concatenate([c_prompt, c_sample], axis=0)
    mod = _ada(c_all, w_ada[0], b_ada[0][None, :])
    mod3 = mod[:, None, :]

    w_r = jnp.concatenate(
        [w_rg[0], w_re[0], jnp.zeros((D, N_ROUTE - N_GROUPS - N_EXPERTS), _F32)], axis=1).T.astype(_BF16)
    b_r = jnp.concatenate(
        [b_rg[0], b_re[0], jnp.zeros((N_ROUTE - N_GROUPS - N_EXPERTS,), _F32)])[:, None]
    w_dw_p = jnp.concatenate([w_dw[0], jnp.zeros((1, D), _F32)], axis=0)
    mix_w = (g_mix[0][None, :], w_in[0].astype(_BF16), w_dw_p, b_dw[0][None, :], g_ln[0][None, :],
             b_ln[0][None, :], w_pw[0].astype(_BF16), w_pool[0].astype(_BF16), s_pool[0][None, :],
             w_out[0].astype(_BF16), g_ffn[0][None, :], w_r, b_r)
    w_gu = jnp.concatenate([w_gate[0], w_up[0]], axis=2).astype(_BF16)
    w_dn = w_down[0].astype(_BF16)
    g_fin = g_final[None, :]

    hc_p = jnp.zeros((bp, CONV_PAD, D), _F32)
    hp_p = jnp.zeros((bp, POOL_PAD, D), _F32)
    hc_s = jnp.pad(state_conv[0], ((0, 0), (CONV_PAD - (CONV_WIDTH - 1), 0), (0, 0)))
    hp_s = jnp.pad(state_pool[0], ((0, 0), (1, 0), (0, 0)))

    y_p, ncp, npp = _stream(x_prompt, mod3, 0, hc_p, hp_p, 0, mix_w, w_gu, w_dn, g_fin,
                            tl=512, rc=256, s_tok=4096, tlf=512)
    y_s, ncs, nps = _stream(x_sample, mod3, bp, hc_s, hp_s, past_len, mix_w, w_gu, w_dn, g_fin,
                            tl=ls, rc=ls, s_tok=bs * ls, tlf=ls)
    return (y_p, y_s, ncp, npp, ncs, nps)
```

```python
import functools

import jax
import jax.numpy as jnp
from jax import lax
from jax.experimental import pallas as pl
from jax.experimental.pallas import tpu as pltpu

D = 1024
N_IN = 5
N_ADA = 6
CONV_WIDTH = 31
CONV_PAD = 32
POOL_PAD = 16
HIST_STRIDE = 2
CONV_ROWS = 64
POOL_WINDOWS = (2, 4, 8, 16)
POOL_GROUP_W = D // 4
N_GROUPS = 4
EPG = 8
N_EXPERTS = N_GROUPS * EPG
N_ROUTE = 40
D_EXPERT = D // 2
EPS = 1e-6
LANES = 128
SUBLANES = 8
ROW_VREGS = D // LANES
MOE_ROWS = 144
MOE_STRIDE = MOE_ROWS + 8
MOE_LEAD = 1024
MOE_DUMP = 8
MOE_EXPERTS_PER_STEP = 2
VMEM_LIMIT = 56 * 1024 * 1024

_F32 = jnp.float32
_BF16 = jnp.bfloat16


def _dot(a, b):
    return jnp.dot(a, b, preferred_element_type=_F32)


def _sigmoid(x):
    return jax.nn.sigmoid(x)


def _sigmoid_eup(x):
    return 0.5 * jnp.tanh(0.5 * x) + 0.5


def _round_bf16(x):
    return x.astype(_BF16).astype(_F32)


def _ada_kernel(c_ref, w_ref, b_ref, o_ref):
    c = c_ref[...]
    s = (c * _sigmoid(c)).astype(_BF16)
    o_ref[...] = _dot(s, w_ref[...].astype(_BF16)) + b_ref[...]


def _ada(c_all, w_ada, b_ada):
    nb = c_all.shape[0]
    return pl.pallas_call(
        _ada_kernel,
        grid=(N_ADA,),
        in_specs=[
            pl.BlockSpec((nb, D), lambda j: (0, 0)),
            pl.BlockSpec((D, D), lambda j: (0, j)),
            pl.BlockSpec((1, D), lambda j: (0, j)),
        ],
        out_specs=pl.BlockSpec((nb, D), lambda j: (0, j)),
        out_shape=jax.ShapeDtypeStruct((nb, N_ADA * D), _F32),
        name="ada",
    )(c_all, w_ada, b_ada)


def _mix_kernel(x_ref, mod_ref, hc_ref, hp_ref, gmix_ref, win_ref, w3_ref, wdw_ref, bdw_ref, gln_ref,
                bln_ref, wpw_ref, wpool_ref, spool_ref, wout_ref, gffn_ref, wr_ref, br_ref,
                x1_ref, n2t_ref, route_ref, ncs_ref, nps_ref,
                nb_scr, vbuf, ubuf, t1, t2, z3, ab_scr, pb_scr, *, tl, rc, start_pos):
    l = pl.program_id(1)
    nchunk = tl // rc
    slab = 3 * D // nchunk

    def chunks(body):
        def step(i, carry):
            body(pl.multiple_of(i * rc, rc))
            return carry
        lax.fori_loop(0, nchunk, step, 0)

    def trow(j, n):
        return pl.ds(HIST_STRIDE * j, n, stride=HIST_STRIDE)

    def hist_put(buf, j, val):
        for lc in range(ROW_VREGS):
            buf[lc, trow(j, val.shape[0]), :] = val[:, lc * LANES:(lc + 1) * LANES]

    def hist_get(buf, j, n):
        return jnp.concatenate([buf[lc, trow(j, n), :] for lc in range(ROW_VREGS)], axis=1)

    @pl.when(l == 0)
    def _():
        hist_put(vbuf, 0, _round_bf16(hc_ref[...]))
        hist_put(ubuf, 0, hp_ref[...])

    sh1 = mod_ref[:, 0 * D:1 * D]
    sc1 = mod_ref[:, 1 * D:2 * D]
    gt1 = mod_ref[:, 2 * D:3 * D]
    sh2 = mod_ref[:, 3 * D:4 * D]
    sc2 = mod_ref[:, 4 * D:5 * D]

    def norm1(r):
        xc = x_ref[pl.ds(r, rc), :]
        ms = jnp.mean(xc * xc, axis=-1, keepdims=True)
        n = xc * lax.rsqrt(ms + EPS) * gmix_ref[...]
        nb_scr[pl.ds(r, rc), :] = (n * (1.0 + sc1) + sh1).astype(_BF16)
    chunks(norm1)

    t1[...] = _dot(nb_scr[...], win_ref[:, 0 * D:1 * D])
    t2[...] = _dot(nb_scr[...], win_ref[:, 1 * D:2 * D])

    def glu(r):
        v = t1[pl.ds(r, rc), :] * _sigmoid_eup(t2[pl.ds(r, rc), :])
        hist_put(vbuf, CONV_PAD + r, _round_bf16(v))
    chunks(glu)
    ncs_ref[...] = t1[tl - CONV_PAD:tl, :] * _sigmoid_eup(t2[tl - CONV_PAD:tl, :])

    def z3_cols(rows, col0, width):
        si, off = divmod(col0, slab)
        assert off + width <= slab
        return z3[si, rows, off:off + width]

    def conv(c, carry):
        r = pl.multiple_of(c * rc, rc)
        for lc in range(ROW_VREGS):
            cs = slice(lc * LANES, (lc + 1) * LANES)
            taps = [jnp.broadcast_to(wdw_ref[k:k + 1, cs], (SUBLANES, LANES)) for k in range(CONV_WIDTH)]
            bias = jnp.broadcast_to(bdw_ref[:, cs], (SUBLANES, LANES))
            ngrp = min(rc, CONV_ROWS) // SUBLANES
            for r2 in range(0, rc, ngrp * SUBLANES):
                accs = [bias] * ngrp
                for o in range((ngrp - 1) * SUBLANES + CONV_WIDTH):
                    xw = vbuf[lc, trow(r + r2 + o + (CONV_PAD - CONV_WIDTH + 1), SUBLANES), :]
                    for i in range(ngrp):
                        k = o - i * SUBLANES
                        if 0 <= k < CONV_WIDTH:
                            accs[i] = accs[i] + taps[k] * xw
                for i in range(ngrp):
                    t1[pl.ds(r + r2 + i * SUBLANES, SUBLANES), cs] = accs[i]
        cv = t1[pl.ds(r, rc), :]
        mu = jnp.mean(cv, axis=-1, keepdims=True)
        xc = cv - mu
        a = xc * lax.rsqrt(jnp.mean(xc * xc, axis=-1, keepdims=True) + EPS) * gln_ref[...] + bln_ref[...]
        ab_scr[pl.ds(r, rc), :] = (a * _sigmoid(a)).astype(_BF16)
        z3[c] = _dot(nb_scr[...], w3_ref[c])
        return carry
    lax.fori_loop(0, nchunk, conv, 0)

    hist_put(ubuf, POOL_PAD, jnp.concatenate(
        [z3_cols(slice(None), q * LANES, LANES) for q in range(ROW_VREGS)], axis=1))
    t1[...] = _dot(ab_scr[...], wpw_ref[...])

    for c in range(nchunk):
        r = c * rc
        pos = start_pos + l * tl + r + lax.broadcasted_iota(jnp.int32, (rc, 1), 0)
        for lc in range(ROW_VREGS):
            cs = slice(lc * LANES, (lc + 1) * LANES)
            w = POOL_WINDOWS[lc * LANES // POOL_GROUP_W]
            u = z3_cols(slice(r, r + rc), lc * LANES, LANES)
            sw = u
            for i in range(1, w):
                sw = sw + ubuf[lc, trow(POOL_PAD + r - i, rc), :]
            cnt = jnp.minimum(pos + 1, w).astype(_F32)
            pb_scr[r:r + rc, cs] = (sw * (1.0 / cnt) - u).astype(_BF16)

    for g in range(N_GROUPS):
        gs = slice(g * POOL_GROUP_W, (g + 1) * POOL_GROUP_W)
        t2[:, gs] = _dot(pb_scr[:, gs], wpool_ref[g]) * spool_ref[:, gs]

    def mix(r):
        rs = pl.ds(r, rc)
        for q in range(D // POOL_GROUP_W):
            qs = slice(q * POOL_GROUP_W, (q + 1) * POOL_GROUP_W)
            ga = z3_cols(rs, D + q * POOL_GROUP_W, POOL_GROUP_W)
            gb = z3_cols(rs, 2 * D + q * POOL_GROUP_W, POOL_GROUP_W)
            ab_scr[rs, qs] = (_sigmoid_eup(ga) * t1[rs, qs] + _sigmoid_eup(gb) * t2[rs, qs]).astype(_BF16)
    chunks(mix)

    t1[...] = _dot(ab_scr[...], wout_ref[...])

    def resid(r):
        rs = pl.ds(r, rc)
        x1 = x_ref[rs, :] + gt1 * t1[rs, :]
        x1_ref[rs, :] = x1
        ms = jnp.mean(x1 * x1, axis=-1, keepdims=True)
        n2 = x1 * lax.rsqrt(ms + EPS) * gffn_ref[...] * (1.0 + sc2) + sh2
        nb_scr[rs, :] = n2.astype(_BF16)
        for j in range(ROW_VREGS):
            n2t_ref[pl.ds(r * ROW_VREGS + j, rc, stride=ROW_VREGS), :] = n2[:, j * LANES:(j + 1) * LANES]
    chunks(resid)

    lg = lax.dot_general(wr_ref[...], nb_scr[...], (((1,), (1,)), ((), ())),
                         preferred_element_type=_F32) + br_ref[...]
    lgg = lg[0:N_GROUPS, :]
    mg = jnp.max(lgg, axis=0, keepdims=True)
    p_g = 1.0 / jnp.sum(jnp.exp(lgg - mg), axis=0, keepdims=True)
    gi = lax.broadcasted_iota(jnp.int32, lgg.shape, 0)
    g_idx = jnp.min(jnp.where(lgg == mg, gi, N_GROUPS), axis=0, keepdims=True)
    sel = jnp.zeros((EPG, tl), _F32)
    for g in range(N_GROUPS):
        sel = jnp.where(g_idx == g, lg[N_GROUPS + g * EPG:N_GROUPS + (g + 1) * EPG, :], sel)
    ms = jnp.max(sel, axis=0, keepdims=True)
    es = jnp.exp(sel - ms)
    pe = es / jnp.sum(es, axis=0, keepdims=True)
    ei = lax.broadcasted_iota(jnp.int32, pe.shape, 0)
    v1 = jnp.max(pe, axis=0, keepdims=True)
    i1 = jnp.min(jnp.where(pe == v1, ei, EPG), axis=0, keepdims=True)
    pe2 = jnp.where(ei == i1, -1.0, pe)
    v2 = jnp.max(pe2, axis=0, keepdims=True)
    i2 = jnp.min(jnp.where(pe2 == v2, ei, EPG), axis=0, keepdims=True)
    scale = p_g / (v1 + v2)
    route_ref[0:1, :] = (g_idx * EPG + i1).astype(_F32)
    route_ref[1:2, :] = (g_idx * EPG + i2).astype(_F32)
    route_ref[2:3, :] = v1 * scale
    route_ref[3:4, :] = v2 * scale
    route_ref[4:8, :] = jnp.zeros((4, tl), _F32)

    utail = hist_get(ubuf, tl, POOL_PAD)
    nps_ref[...] = utail
    hist_put(vbuf, 0, hist_get(vbuf, tl, CONV_PAD))
    hist_put(ubuf, 0, utail)


def _const_spec(shape):
    nd = len(shape)
    return pl.BlockSpec(shape, lambda b, l: (0,) * nd, pipeline_mode=pl.Buffered(1))


def _mixer(x, mod3, boff, hc, hp, start_pos, wts, tl, rc):
    bsz, seq, _ = x.shape
    nl = seq // tl
    nchunk = tl // rc
    slab = 3 * D // nchunk
    w_in = wts[1]
    w3 = jnp.transpose(w_in[:, 2 * D:].reshape(D, nchunk, slab), (1, 0, 2))
    wts = (wts[0], w_in[:, :2 * D], w3) + tuple(wts[2:])
    kern = functools.partial(_mix_kernel, tl=tl, rc=rc, start_pos=start_pos)
    in_specs = [
        pl.BlockSpec((None, tl, D), lambda b, l: (b, l, 0)),
        pl.BlockSpec((None, 1, N_ADA * D), lambda b, l: (b + boff, 0, 0)),
        pl.BlockSpec((None, CONV_PAD, D), lambda b, l: (b, 0, 0)),
        pl.BlockSpec((None, POOL_PAD, D), lambda b, l: (b, 0, 0)),
    ] + [_const_spec(w.shape) for w in wts]
    out_specs = [
        pl.BlockSpec((None, tl, D), lambda b, l: (b, l, 0)),
        pl.BlockSpec((tl * ROW_VREGS, LANES), lambda b, l: (b * nl + l, 0)),
        pl.BlockSpec((None, SUBLANES, tl), lambda b, l: (b, 0, l)),
        pl.BlockSpec((None, CONV_PAD, D), lambda b, l: (b, 0, 0)),
        pl.BlockSpec((None, POOL_PAD, D), lambda b, l: (b, 0, 0)),
    ]
    out_shape = [
        jax.ShapeDtypeStruct((bsz, seq, D), _F32),
        jax.ShapeDtypeStruct((bsz * seq * ROW_VREGS, LANES), _F32),
        jax.ShapeDtypeStruct((bsz, SUBLANES, seq), _F32),
        jax.ShapeDtypeStruct((bsz, CONV_PAD, D), _F32),
        jax.ShapeDtypeStruct((bsz, POOL_PAD, D), _F32),
    ]
    scratch = [
        pltpu.VMEM((tl, D), _BF16),
        pltpu.VMEM((ROW_VREGS, HIST_STRIDE * (tl + CONV_PAD), LANES), _F32),
        pltpu.VMEM((ROW_VREGS, HIST_STRIDE * (tl + POOL_PAD), LANES), _F32),
        pltpu.VMEM((tl, D), _F32),
        pltpu.VMEM((tl, D), _F32),
        pltpu.VMEM((nchunk, tl, slab), _F32),
        pltpu.VMEM((tl, D), _BF16),
        pltpu.VMEM((tl, D), _BF16),
    ]
    return pl.pallas_call(
        kern,
        grid=(bsz, nl),
        in_specs=in_specs,
        out_specs=out_specs,
        out_shape=out_shape,
        scratch_shapes=scratch,
        compiler_params=pltpu.CompilerParams(
            dimension_semantics=("arbitrary", "arbitrary"), vmem_limit_bytes=VMEM_LIMIT),
        name="mixer",
    )(x, mod3, hc, hp, *wts)


def _moe_kernel(nblk_ref, bstart_ref, tok_ref, wt_ref, src_ref, wgu_ref, wdn_ref, x1_ref, mod_ref, g_ref,
                y_ref, acc, xc_scr, yc_scr, xb_scr, *, s_tok, tlf, unroll):
    s = pl.program_id(0)
    e = pl.program_id(1)
    n_esteps = N_EXPERTS // MOE_EXPERTS_PER_STEP

    def arow(t):
        return pl.ds(pl.multiple_of(t * ROW_VREGS, ROW_VREGS), ROW_VREGS)

    def gather(blk):
        base = MOE_LEAD + blk * MOE_ROWS
        for mi in range(MOE_ROWS):
            t = jnp.minimum(tok_ref[base + mi], s_tok - 1)
            xc_scr[pl.ds(mi, ROW_VREGS, stride=MOE_STRIDE), :] = src_ref[t]

    def scatter(blk):
        base = MOE_LEAD + blk * MOE_ROWS
        for g0 in range(0, MOE_ROWS, unroll):
            ts = [tok_ref[base + g0 + i] for i in range(unroll)]
            vals = [acc[arow(ts[i]), :] + wt_ref[base + g0 + i]
                    * yc_scr[pl.ds(g0 + i, ROW_VREGS, stride=MOE_STRIDE), :] for i in range(unroll)]
            for i in range(unroll):
                acc[arow(ts[i]), :] = vals[i]

    @pl.when(e == 0)
    def _():
        acc[...] = jnp.zeros(acc.shape, _F32)
        yc_scr[...] = jnp.zeros(yc_scr.shape, _F32)
        gather(0)

    def block(ee, blk):
        xb_scr[...] = jnp.concatenate(
            [xc_scr[pl.ds(j * MOE_STRIDE, MOE_ROWS), :] for j in range(ROW_VREGS)], axis=1).astype(_BF16)
        gather(blk + 1)
        scatter(blk - 1)
        y = None
        for hs in (slice(0, D_EXPERT // 2), slice(D_EXPERT // 2, D_EXPERT)):
            g = _dot(xb_scr[...], wgu_ref[ee, :, hs])
            u = _dot(xb_scr[...], wgu_ref[ee, :, D_EXPERT + hs.start:D_EXPERT + hs.stop])
            act = (g * _sigmoid(g) * u).astype(_BF16)
            yh = _dot(act, wdn_ref[ee, hs, :])
            y = yh if y is None else y + yh
        for j in range(ROW_VREGS):
            yc_scr[pl.ds(j * MOE_STRIDE, MOE_ROWS), :] = y[:, j * LANES:(j + 1) * LANES]

    for ee in range(MOE_EXPERTS_PER_STEP):
        idx = s * N_EXPERTS + jnp.minimum(e, n_esteps - 1) * MOE_EXPERTS_PER_STEP + ee
        nb = jnp.where(e < n_esteps, nblk_ref[idx], 0)
        b0 = bstart_ref[idx]

        def body(b, carry, ee=ee, b0=b0):
            block(ee, b0 + b)
            return carry
        lax.fori_loop(0, nb, body, 0)

    @pl.when(e == n_esteps - 1)
    def _():
        scatter(b0 + nb - 1)

    @pl.when(e >= n_esteps)
    def _():
        row0 = (e - n_esteps) * tlf
        rcf = min(tlf, 128)

        def body(i, carry):
            r = pl.multiple_of(i * rcf, rcf)
            f = jnp.concatenate(
                [acc[pl.ds((row0 + r) * ROW_VREGS + j, rcf, stride=ROW_VREGS), :] for j in range(ROW_VREGS)],
                axis=1)
            x2 = x1_ref[pl.ds(r, rcf), :] + mod_ref[...] * f
            ms = jnp.mean(x2 * x2, axis=-1, keepdims=True)
            y_ref[pl.ds(r, rcf), :] = x2 * lax.rsqrt(ms + EPS) * g_ref[...]
            return carry
        lax.fori_loop(0, tlf // rcf, body, 0)


def _moe(nblk, bstart, tok, wt, n2t, w_gu, w_dn, x1, mod3, boff, g_final, s_tok, tlf, p_max):
    bsz, seq, _ = x1.shape
    n_tok = bsz * seq
    n_super = n_tok // s_tok
    n_esteps = N_EXPERTS // MOE_EXPERTS_PER_STEP
    per = s_tok // tlf
    kern = functools.partial(_moe_kernel, s_tok=s_tok, tlf=tlf, unroll=8)

    def estep(e):
        return jnp.minimum(e, n_esteps - 1)

    def ostep(s, e):
        return s * per + jnp.maximum(e - n_esteps, 0)

    grid_spec = pltpu.PrefetchScalarGridSpec(
        num_scalar_prefetch=2,
        grid=(n_super, n_esteps + per),
        in_specs=[
            pl.BlockSpec((p_max,), lambda s, e, *_: (s,), memory_space=pltpu.SMEM),
            pl.BlockSpec((p_max,), lambda s, e, *_: (s,), memory_space=pltpu.SMEM),
            pl.BlockSpec((s_tok, ROW_VREGS, LANES), lambda s, e, *_: (s, 0, 0),
                         pipeline_mode=pl.Buffered(1)),
            pl.BlockSpec((MOE_EXPERTS_PER_STEP, D, D), lambda s, e, *_: (estep(e), 0, 0)),
            pl.BlockSpec((MOE_EXPERTS_PER_STEP, D_EXPERT, D), lambda s, e, *_: (estep(e), 0, 0)),
            pl.BlockSpec((tlf, D), lambda s, e, *_: (ostep(s, e), 0)),
            pl.BlockSpec((None, 1, D), lambda s, e, *_: (ostep(s, e) * tlf // seq + boff, 0, N_ADA - 1)),
            pl.BlockSpec((1, D), lambda s, e, *_: (0, 0)),
        ],
        out_specs=pl.BlockSpec((tlf, D), lambda s, e, *_: (ostep(s, e), 0)),
        scratch_shapes=[
            pltpu.VMEM(((s_tok + MOE_DUMP) * ROW_VREGS, LANES), _F32),
            pltpu.VMEM((ROW_VREGS * MOE_STRIDE, LANES), _F32),
            pltpu.VMEM((ROW_VREGS * MOE_STRIDE, LANES), _F32),
            pltpu.VMEM((MOE_ROWS, D), _BF16),
        ],
    )
    y = pl.pallas_call(
        kern,
        grid_spec=grid_spec,
        out_shape=jax.ShapeDtypeStruct((n_tok, D), _F32),
        compiler_params=pltpu.CompilerParams(
            dimension_semantics=("arbitrary", "arbitrary"), vmem_limit_bytes=VMEM_LIMIT),
        name="moe",
    )(nblk, bstart, tok, wt, n2t, w_gu, w_dn, x1.reshape(n_tok, D), mod3, g_final)
    return y.reshape(bsz, seq, D)


def _dispatch(route, s_tok):
    bsz, _, seq = route.shape
    n_tok = bsz * seq
    n_super = n_tok // s_tok
    r = jnp.transpose(route[:, 0:4, :], (1, 0, 2)).reshape(4, n_super, s_tok)
    e_real = jnp.concatenate([r[0], r[1]], axis=1).astype(jnp.int32)
    w_real = jnp.concatenate([r[2], r[3]], axis=1)
    t_real = jnp.broadcast_to(jnp.tile(jnp.arange(s_tok, dtype=jnp.int32), 2)[None, :], e_real.shape)
    experts = jnp.arange(N_EXPERTS, dtype=jnp.int32)
    counts = jnp.sum(e_real[:, :, None] == experts, axis=1).astype(jnp.int32)
    n_pad = (-counts) % MOE_ROWS
    cand = jnp.arange(MOE_ROWS - 1, dtype=jnp.int32)
    key_pad = jnp.where(cand[None, None, :] < n_pad[:, :, None], experts[None, :, None], N_EXPERTS)
    key_pad = key_pad.reshape(n_super, N_EXPERTS * (MOE_ROWS - 1)).astype(jnp.int32)
    n_sort = 2 * s_tok + N_EXPERTS * (MOE_ROWS - 1)
    p_alloc = (n_sort + MOE_ROWS + 1023) // 1024 * 1024
    tok_bits = s_tok.bit_length()
    keys = jnp.concatenate([(e_real << tok_bits) | t_real, (key_pad << tok_bits) | s_tok], axis=1)
    wts = jnp.concatenate([w_real, jnp.zeros((n_super, n_sort - 2 * s_tok), _F32)], axis=1)
    keys, wt = lax.sort((keys, wts), dimension=1, num_keys=1)
    tok = keys & ((1 << tok_bits) - 1)
    tok = jnp.pad(tok, ((0, 0), (MOE_LEAD, p_alloc - n_sort)), constant_values=s_tok)
    wt = jnp.pad(wt, ((0, 0), (MOE_LEAD, p_alloc - n_sort)))
    p_alloc += MOE_LEAD
    pcounts = counts + n_pad
    pstarts = jnp.cumsum(pcounts, axis=1) - pcounts
    nblk = (pcounts // MOE_ROWS).reshape(-1)
    bstart = (pstarts // MOE_ROWS).reshape(-1)
    return nblk, bstart, tok.reshape(-1), wt.reshape(-1), p_alloc


def _stream(x, mod3, boff, hc, hp, start_pos, mix_w, w_gu, w_dn, g_final, tl, rc, s_tok, tlf):
    bsz, seq, _ = x.shape
    x1, n2t, route, ncs, nps = _mixer(x, mod3, boff, hc, hp, start_pos, mix_w, tl, rc)
    nblk, bstart, tok, wt, p_alloc = _dispatch(route, s_tok)
    n2t3 = n2t.reshape(bsz * seq, ROW_VREGS, LANES)
    y = _moe(nblk, bstart, tok, wt, n2t3, w_gu, w_dn, x1, mod3, boff, g_final, s_tok, tlf, p_alloc)
    return y, ncs[None, :, CONV_PAD - (CONV_WIDTH - 1):, :], nps[None, :, 1:, :]


def kernel(x_prompt, x_sample, state_conv, state_pool, c_prompt, c_sample, w_ada, b_ada, g_mix, w_in, w_dw, b_dw, g_ln, b_ln, w_pw, w_pool, s_pool, w_out, g_ffn, w_rg, b_rg, w_re, b_re, w_gate, w_up, w_down, g_final):
    bp, lp, _ = x_prompt.shape
    bs, ls, _ = x_sample.shape
    past_len = 1024

    c_all = jnp.concatenate([c_prompt, c_sample], axis=0)
    mod = _ada(c_all, w_ada[0], b_ada[0][None, :])
    mod3 = mod[:, None, :]

    w_r = jnp.concatenate(
        [w_rg[0], w_re[0], jnp.zeros((D, N_ROUTE - N_GROUPS - N_EXPERTS), _F32)], axis=1).T.astype(_BF16)
    b_r = jnp.concatenate(
        [b_rg[0], b_re[0], jnp.zeros((N_ROUTE - N_GROUPS - N_EXPERTS,), _F32)])[:, None]
    w_dw_p = jnp.concatenate([w_dw[0], jnp.zeros((1, D), _F32)], axis=0)
    mix_w = (g_mix[0][None, :], w_in[0].astype(_BF16), w_dw_p, b_dw[0][None, :], g_ln[0][None, :],
             b_ln[0][None, :], w_pw[0].astype(_BF16), w_pool[0].astype(_BF16), s_pool[0][None, :],
             w_out[0].astype(_BF16), g_ffn[0][None, :], w_r, b_r)
    w_gu = jnp.concatenate([w_gate[0], w_up[0]], axis=2).astype(_BF16)
    w_dn = w_down[0].astype(_BF16)
    g_fin = g_final[None, :]

    hc_p = jnp.zeros((bp, CONV_PAD, D), _F32)
    hp_p = jnp.zeros((bp, POOL_PAD, D), _F32)
    hc_s = jnp.pad(state_conv[0], ((0, 0), (CONV_PAD - (CONV_WIDTH - 1), 0), (0, 0)))
    hp_s = jnp.pad(state_pool[0], ((0, 0), (1, 0), (0, 0)))

    y_p, ncp, npp = _stream(x_prompt, mod3, 0, hc_p, hp_p, 0, mix_w, w_gu, w_dn, g_fin,
                            tl=512, rc=512, s_tok=4096, tlf=512)
    y_s, ncs, nps = _stream(x_sample, mod3, bp, hc_s, hp_s, past_len, mix_w, w_gu, w_dn, g_fin,
                            tl=ls, rc=ls, s_tok=bs * ls, tlf=ls)
    return (y_p, y_s, ncp, npp, ncs, nps)
```
